```python
import jax, jax.numpy as jnp
from jax import lax
import numpy as np

D_MODEL = 1024
BATCH = 4
SEQ = 8192
DEPTH = 1

CHUNK = 64
HEAD_DIM = 64
CONV_GROUPS = 8
CONV_DIM = CONV_GROUPS * HEAD_DIM
RWKV_HEADS = 8
RWKV_DIM = RWKV_HEADS * HEAD_DIM
MIX_DIM = CONV_DIM + RWKV_DIM
CONV_K = 3
DECAY_RANK = 64
ICLR_RANK = 64
GATE_RANK = 160
D_FF = 2816
RWKV_COLS = 3 * RWKV_DIM + DECAY_RANK + ICLR_RANK + GATE_RANK
IN_COLS = 3 * CONV_DIM + RWKV_COLS
RMS_EPS = 1e-6
GN_EPS = HEAD_DIM * 1e-5

kernel_name = 'hybrid_shortconv_rwkv7_convffn_block'


def _rms_norm(z, g):
    zf = z.astype(jnp.float32)
    zf = zf * lax.rsqrt(jnp.mean(zf * zf, axis=-1, keepdims=True) + RMS_EPS)
    return (zf * g.astype(jnp.float32)).astype(z.dtype)


def _causal_dwconv(z, w):
    c = z.shape[-1]
    return lax.conv_general_dilated(
        z, w[:, None, :].astype(z.dtype), window_strides=(1,),
        padding=[(CONV_K - 1, 0)], dimension_numbers=('NWC', 'WIO', 'NWC'),
        feature_group_count=c)


def _token_shift(z):
    return jnp.pad(z[:, :-1], ((0, 0), (1, 0), (0, 0)))


def _wkv7(r, decay, k, v, kk, a):
    b, t, h, n = r.shape
    n_chunks = t // CHUNK

    def to_chunks(z):
        return z.reshape(b, n_chunks, CHUNK, h, n).transpose(1, 2, 0, 3, 4)

    def step(S, inp):
        r_t, w_t, k_t, v_t, kk_t, a_t = inp
        sa = jnp.einsum('bhvk,bhk->bhv', S, -kk_t)
        S = (S * w_t[:, :, None, :] + sa[..., None] * (kk_t * a_t)[:, :, None, :]
             + v_t[..., None] * k_t[:, :, None, :])
        return S, jnp.einsum('bhvk,bhk->bhv', S, r_t)

    def chunk_step(S, inp):
        return lax.scan(step, S, inp)

    S0 = jnp.zeros((b, h, n, n), jnp.float32)
    xs = tuple(to_chunks(z) for z in (r, decay, k, v, kk, a))
    _, y = lax.scan(chunk_step, S0, xs)
    return y.transpose(2, 0, 1, 3, 4).reshape(b, t, h, n)


def _head_group_norm(y, wgt, bias):
    mu = jnp.mean(y, axis=-1, keepdims=True)
    var = jnp.mean(jnp.square(y - mu), axis=-1, keepdims=True)
    yn = (y - mu) * lax.rsqrt(var + GN_EPS)
    b, t, h, n = y.shape
    return yn.reshape(b, t, h * n) * wgt.astype(jnp.float32) + bias.astype(jnp.float32)


def _token_mixer(xn, w_in, conv_a_w, shift_mu, w0, w2, a0, a2, g2, k_k, k_a, r_k,
                 lnx_w, lnx_b, w_out):
    b, t, _ = xn.shape
    p = jnp.einsum('btd,de->bte', xn, w_in)
    gB, gC, hA = jnp.split(p[..., :3 * CONV_DIM], 3, axis=-1)
    y_a = gC * _causal_dwconv(gB * hA, conv_a_w)
    q = p[..., 3 * CONV_DIM:]
    q = q + (_token_shift(q) - q) * shift_mu
    splits = np.cumsum([RWKV_DIM, RWKV_DIM, RWKV_DIM, DECAY_RANK, ICLR_RANK]).tolist()
    r, k, v, wd, ad, gd = jnp.split(q, splits, axis=-1)
    wdf = wd.astype(jnp.float32)
    w_log = -jax.nn.softplus(-(w0.astype(jnp.float32)
                               + jnp.tanh(wdf) @ w2.astype(jnp.float32))) - 0.5
    decay = jnp.exp(-jnp.exp(w_log))
    a = jax.nn.sigmoid(a0 + ad @ a2)
    g = jax.nn.sigmoid(gd) @ g2
    heads = lambda z: z.astype(jnp.float32).reshape(b, t, RWKV_HEADS, HEAD_DIM)
    pheads = lambda z: z.astype(jnp.float32).reshape(RWKV_HEADS, HEAD_DIM)
    r4, k4, v4, a4, d4 = heads(r), heads(k), heads(v), heads(a), heads(decay)
    kk = k4 * pheads(k_k)
    kk = kk / jnp.maximum(jnp.linalg.norm(kk, axis=-1, keepdims=True), 1e-12)
    k4 = k4 * (1.0 + (a4 - 1.0) * pheads(k_a))
    y = _wkv7(r4, d4, k4, v4, kk, a4)
    y_b = _head_group_norm(y, lnx_w, lnx_b)
    bonus = jnp.sum(r4 * k4 * r_k.astype(jnp.float32), axis=-1, keepdims=True) * v4
    y_b = (y_b + bonus.reshape(b, t, RWKV_DIM)).astype(xn.dtype) * g
    return jnp.einsum('bte,ed->btd', jnp.concatenate([y_a, y_b], axis=-1), w_out)


def _channel_mixer(hn, w_up, ffn_conv_w, ffn_conv_b, w_down):
    f = jnp.einsum('btd,df->btf', hn, w_up)
    f = _causal_dwconv(f, ffn_conv_w) + ffn_conv_b
    gate, up = jnp.split(f, 2, axis=-1)
    return jnp.einsum('btf,fd->btd', jax.nn.gelu(gate) * up, w_down)


def setup_inputs(seed: int = 0) -> dict:
    key = jax.random.key(seed)
    ks = iter(jax.random.split(key, 32))
    nrm = lambda shape, s: jax.random.normal(next(ks), shape, jnp.float32) * s
    L = DEPTH
    return {
        'x': jax.random.normal(next(ks), (BATCH, SEQ, D_MODEL), jnp.float32),
        'pre_mix_g': 1.0 + nrm((L, D_MODEL), 0.02),
        'w_in': nrm((L, D_MODEL, IN_COLS), D_MODEL ** -0.5),
        'conv_a_w': nrm((L, CONV_K, CONV_DIM), 0.5),
        'shift_mu': jax.random.uniform(next(ks), (L, RWKV_COLS), jnp.float32),
        'w0': jax.random.uniform(next(ks), (L, RWKV_DIM), jnp.float32, -6.0, 0.0),
        'w2': nrm((L, DECAY_RANK, RWKV_DIM), 0.1),
        'a0': nrm((L, RWKV_DIM), 0.1),
        'a2': nrm((L, ICLR_RANK, RWKV_DIM), ICLR_RANK ** -0.5),
        'g2': nrm((L, GATE_RANK, RWKV_DIM), GATE_RANK ** -0.5),
        'k_k': 0.85 + nrm((L, RWKV_DIM), 0.02),
        'k_a': 1.0 + nrm((L, RWKV_DIM), 0.02),
        'r_k': nrm((L, RWKV_HEADS, HEAD_DIM), 0.1),
        'lnx_w': 1.0 + nrm((L, RWKV_DIM), 0.02),
        'lnx_b': nrm((L, RWKV_DIM), 0.01),
        'w_out': nrm((L, MIX_DIM, D_MODEL), MIX_DIM ** -0.5),
        'post_mix_g': 1.0 + nrm((L, D_MODEL), 0.02),
        'pre_ffn_g': 1.0 + nrm((L, D_MODEL), 0.02),
        'w_up': nrm((L, D_MODEL, 2 * D_FF), D_MODEL ** -0.5),
        'ffn_conv_w': nrm((L, CONV_K, 2 * D_FF), 0.5),
        'ffn_conv_b': nrm((L, 2 * D_FF), 0.01),
        'w_down': nrm((L, D_FF, D_MODEL), D_FF ** -0.5),
        'post_ffn_g': 1.0 + nrm((L, D_MODEL), 0.02),
    }


def reference(x, pre_mix_g, w_in, conv_a_w, shift_mu, w0, w2, a0, a2, g2, k_k, k_a, r_k,
              lnx_w, lnx_b, w_out, post_mix_g, pre_ffn_g, w_up, ffn_conv_w, ffn_conv_b,
              w_down, post_ffn_g):
    h = x
    for l in range(DEPTH):
        mix = _token_mixer(_rms_norm(h, pre_mix_g[l]), w_in[l], conv_a_w[l], shift_mu[l],
                           w0[l], w2[l], a0[l], a2[l], g2[l], k_k[l], k_a[l], r_k[l],
                           lnx_w[l], lnx_b[l], w_out[l])
        h = h + _rms_norm(mix, post_mix_g[l])
        ffn = _channel_mixer(_rms_norm(h, pre_ffn_g[l]), w_up[l], ffn_conv_w[l],
                             ffn_conv_b[l], w_down[l])
        h = h + _rms_norm(ffn, post_ffn_g[l])
    return h
```

```python
import functools
import math

import jax
import jax.numpy as jnp
from jax import lax
from jax.experimental import pallas as pl
from jax.experimental.pallas import tpu as pltpu

D_MODEL = 1024
HEAD_DIM = 64
N_HEADS = 8
CONV_DIM = 512
RWKV_DIM = 512
DECAY_RANK = 64
ICLR_RANK = 64
GATE_RANK = 160
GATE_PAD = 256
D_FF = 2816
RMS_EPS = 1e-6
GN_EPS = HEAD_DIM * 1e-5

SUBLANES = 8
CHUNK = 64
TM_IN = 256
TT_WKV = 512
TM_OUT = 256
FF_COLS = 256
VMEM_LIMIT = 56 * 1024 * 1024

F32 = jnp.float32
BF16 = jnp.bfloat16
HIGHEST = lax.Precision.HIGHEST


def _dot(a, b):
    return jnp.dot(a, b, preferred_element_type=F32)


def _dot_nt(a, b):
    return lax.dot_general(a, b, (((1,), (1,)), ((), ())), preferred_element_type=F32)


def _dot_tn(a, b):
    return lax.dot_general(a, b, (((0,), (0,)), ((), ())), preferred_element_type=F32)


def _rms(z, g):
    return z * lax.rsqrt(jnp.mean(z * z, axis=-1, keepdims=True) + RMS_EPS) * g


def _sigmoid(z):
    return 1.0 / (1.0 + jnp.exp(-z))


def _shift_rows(z, prev, s):
    zs = pltpu.roll(z, s, 0)
    ps = pltpu.roll(prev, s, 0)
    row = lax.broadcasted_iota(jnp.int32, prev.shape, 0)
    head = jnp.where(row < s, ps, zs[:SUBLANES])
    return jnp.concatenate([head, zs[SUBLANES:]], axis=0)


def _mix_in_kernel(x_ref, gpre_ref, wa_ref, wb_ref, wc_ref, mub_ref, muc_ref, cw_ref, w0_ref,
                   a0_ref, w2a2_ref, g2_ref, kk_ref, ka_ref, hsum_ref,
                   ya_ref, r_ref, k_ref, v_ref, kkn_ref, a_ref, lw_ref, g_ref,
                   cu_ref, cqb_ref, cqc_ref):
    tm = x_ref.shape[1]

    @pl.when(pl.program_id(1) == 0)
    def _():
        cu_ref[...] = jnp.zeros_like(cu_ref)
        cqb_ref[...] = jnp.zeros_like(cqb_ref)
        cqc_ref[...] = jnp.zeros_like(cqc_ref)

    xn = _rms(x_ref[0], gpre_ref[...]).astype(BF16)

    pa = _dot(xn, wa_ref[...])
    u = pa[:, :CONV_DIM] * pa[:, 2 * CONV_DIM:]
    cu = cu_ref[...]
    cw = cw_ref[...]
    conv = (cw[0:1] * _shift_rows(u, cu, 2) + cw[1:2] * _shift_rows(u, cu, 1) + cw[2:3] * u)
    ya_ref[0] = pa[:, CONV_DIM:2 * CONV_DIM] * conv
    cu_ref[...] = u[tm - SUBLANES:]

    qb = _dot(xn, wb_ref[...])
    qbs = _shift_rows(qb, cqb_ref[...], 1)
    cqb_ref[...] = qb[tm - SUBLANES:]
    qb = qb + (qbs - qb) * mub_ref[...]
    qc = _dot(xn, wc_ref[...])
    qcs = _shift_rows(qc, cqc_ref[...], 1)
    cqc_ref[...] = qc[tm - SUBLANES:]
    qc = qc + (qcs - qc) * muc_ref[...]

    r = qb[:, :RWKV_DIM]
    k = qb[:, RWKV_DIM:2 * RWKV_DIM]
    v = qb[:, 2 * RWKV_DIM:]
    da = qc[:, :DECAY_RANK + ICLR_RANK]
    lane = lax.broadcasted_iota(jnp.int32, da.shape, 1)
    da = jnp.where(lane < DECAY_RANK, jnp.tanh(da), da)
    wa2 = _dot(da.astype(BF16), w2a2_ref[...])
    lw = -math.exp(-0.5) * _sigmoid(w0_ref[...] + wa2[:, :RWKV_DIM])
    a = _sigmoid(a0_ref[...] + wa2[:, RWKV_DIM:])
    g = _dot(_sigmoid(qc[:, DECAY_RANK + ICLR_RANK:]).astype(BF16), g2_ref[...])

    kk = k * kk_ref[...]
    ss = jnp.dot(kk * kk, hsum_ref[...], precision=HIGHEST, preferred_element_type=F32)
    kkn = kk * lax.rsqrt(jnp.maximum(ss, 1e-24))
    k = k * (1.0 + (a - 1.0) * ka_ref[...])

    r_ref[0] = r
    k_ref[0] = k
    v_ref[0] = v
    kkn_ref[0] = kkn
    a_ref[0] = a
    lw_ref[0] = lw
    g_ref[0] = g


def _mix_in(x, gpre, wa, wb, wc, mub, muc, cw, w0, a0, w2a2, g2p, k_k, k_a, hsum):
    b, t, d = x.shape
    tm = TM_IN
    grid = (b, t // tm)
    tok = lambda n: pl.BlockSpec((1, tm, n), lambda i, j: (i, j, 0))
    full = lambda arr: pl.BlockSpec(arr.shape, lambda i, j: (0,) * arr.ndim)
    params = (gpre, wa, wb, wc, mub, muc, cw, w0, a0, w2a2, g2p, k_k, k_a, hsum)
    out_sds = jax.ShapeDtypeStruct((b, t, RWKV_DIM), F32)
    return pl.pallas_call(
        _mix_in_kernel,
        grid=grid,
        in_specs=[tok(d)] + [full(p) for p in params],
        out_specs=[tok(RWKV_DIM)] * 8,
        out_shape=[out_sds] * 8,
        scratch_shapes=[pltpu.VMEM((SUBLANES, CONV_DIM), F32),
                        pltpu.VMEM((SUBLANES, 3 * RWKV_DIM), F32),
                        pltpu.VMEM((SUBLANES, wc.shape[1]), F32)],
        compiler_params=pltpu.CompilerParams(
            dimension_semantics=("arbitrary", "arbitrary"), vmem_limit_bytes=VMEM_LIMIT),
        name="mix_in",
    )(x, *params)


def _wkv_kernel(r_ref, k_ref, v_ref, kk_ref, a_ref, lw_ref, g_ref, lnw_ref, lnb_ref, rk_ref,
                yb_ref, s_ref):
    tt = r_ref.shape[1]
    c = CHUNK

    @pl.when(pl.program_id(1) == 0)
    def _():
        s_ref[...] = jnp.zeros_like(s_ref)

    row = lax.broadcasted_iota(jnp.int32, (c, c), 0)
    col = lax.broadcasted_iota(jnp.int32, (c, c), 1)
    incl = row >= col
    strict = row > col
    tri = incl.astype(F32)
    eye = (row == col).astype(F32)

    def chunk_body(ci, carry):
        rows = pl.ds(pl.multiple_of(ci * c, c), c)
        r = r_ref[0, rows, :]
        k = k_ref[0, rows, :]
        v = v_ref[0, rows, :]
        kk = kk_ref[0, rows, :]
        a = a_ref[0, rows, :]
        lw = lw_ref[0, rows, :]
        g = g_ref[0, rows, :]

        lcum = jnp.dot(tri, lw, precision=HIGHEST, preferred_element_type=F32)
        lend = lcum[c - 1:c, :]
        e_pos = jnp.exp(lcum)
        e_neg = jnp.exp(-lcum)
        e_end = jnp.exp(lend - lcum)
        beta = kk * a
        r_t = (r * e_pos).astype(BF16)
        a_t = (-kk * jnp.exp(lcum - lw)).astype(BF16)
        k_t = (k * e_neg).astype(BF16)
        b_t = (beta * e_neg).astype(BF16)
        b_e = (beta * e_end).astype(BF16)
        k_e = (k * e_end).astype(BF16)
        p_end = jnp.exp(lend)
        vb = v.astype(BF16)
        rkr = r * k * rk_ref[...]

        outs = []
        for h in range(N_HEADS):
            sl = slice(h * HEAD_DIM, (h + 1) * HEAD_DIM)
            lhs = jnp.concatenate([a_t[:, sl], r_t[:, sl]], axis=0)
            ab = _dot_nt(lhs, b_t[:, sl])
            ak = _dot_nt(lhs, k_t[:, sl])
            a_ab = jnp.where(strict, ab[:c], 0.0)
            a_rb = jnp.where(incl, ab[c:], 0.0).astype(BF16)
            a_ak = jnp.where(strict, ak[:c], 0.0).astype(BF16)
            a_rk = jnp.where(incl, ak[c:], 0.0).astype(BF16)
            pw = a_ab
            tinv = eye + a_ab
            for _ in range(int(math.log2(c)) - 1):
                pwb = pw.astype(BF16)
                pw = _dot(pwb, pwb)
                tinv = tinv + _dot(tinv.astype(BF16), pw.astype(BF16))
            tb = tinv.astype(BF16)
            vh = vb[:, sl]
            w_m = _dot(tb, a_t[:, sl])
            u_m = _dot(tb, _dot(a_ak, vh).astype(BF16))
            s = s_ref[h]
            sb = s.astype(BF16)
            u = _dot_nt(w_m.astype(BF16), sb) + u_m
            ub = u.astype(BF16)
            y = _dot_nt(r_t[:, sl], sb) + _dot(a_rb, ub) + _dot(a_rk, vh)
            s_ref[h] = s * p_end[:, sl] + _dot_tn(ub, b_e[:, sl]) + _dot_tn(vh, k_e[:, sl])

            mu = jnp.mean(y, axis=-1, keepdims=True)
            yc = y - mu
            var = jnp.mean(yc * yc, axis=-1, keepdims=True)
            yn = yc * lax.rsqrt(var + GN_EPS) * lnw_ref[:, sl] + lnb_ref[:, sl]
            bonus = jnp.sum(rkr[:, sl], axis=-1, keepdims=True) * v[:, sl]
            outs.append((yn + bonus) * g[:, sl])
        yb_ref[0, rows, :] = jnp.concatenate(outs, axis=1)
        return carry

    lax.fori_loop(0, tt // c, chunk_body, 0)


def _wkv(r, k, v, kkn, a, lw, g, lnw, lnb, rk):
    b, t, n = r.shape
    tt = TT_WKV
    tok = pl.BlockSpec((1, tt, n), lambda i, j: (i, j, 0))
    vec = pl.BlockSpec((1, n), lambda i, j: (0, 0))
    return pl.pallas_call(
        _wkv_kernel,
        grid=(b, t // tt),
        in_specs=[tok] * 7 + [vec] * 3,
        out_specs=tok,
        out_shape=jax.ShapeDtypeStruct((b, t, n), F32),
        scratch_shapes=[pltpu.VMEM((N_HEADS, HEAD_DIM, HEAD_DIM), F32)],
        compiler_params=pltpu.CompilerParams(
            dimension_semantics=("arbitrary", "arbitrary"), vmem_limit_bytes=VMEM_LIMIT),
        name="wkv",
    )(r, k, v, kkn, a, lw, g, lnw, lnb, rk)


def _gelu_tanh(z):
    return 0.5 * z * (1.0 + jnp.tanh(math.sqrt(2.0 / math.pi) * (z + 0.044715 * (z * z * z))))


def _out_ffn_kernel(x_ref, ya_ref, yb_ref, wout_ref, gpost_ref, gpre2_ref, wup_ref, fcw_ref,
                    fcb_ref, wdown_ref, gpost2_ref, o_ref, cf_ref):
    tm = x_ref.shape[1]

    @pl.when(pl.program_id(1) == 0)
    def _():
        cf_ref[...] = jnp.zeros_like(cf_ref)

    ycat = jnp.concatenate([ya_ref[0], yb_ref[0]], axis=1).astype(BF16)
    mix = _dot(ycat, wout_ref[...])
    h = x_ref[0] + _rms(mix, gpost_ref[...])
    hn = _rms(h, gpre2_ref[...]).astype(BF16)

    def conv_cols(lo):
        cols = slice(lo, lo + FF_COLS)
        f = _dot(hn, wup_ref[:, cols])
        prev = cf_ref[:, cols]
        cw = fcw_ref[:, cols]
        out = (cw[0:1] * _shift_rows(f, prev, 2) + cw[1:2] * _shift_rows(f, prev, 1)
               + cw[2:3] * f + fcb_ref[:, cols])
        cf_ref[:, cols] = f[tm - SUBLANES:]
        return out

    acc = jnp.zeros((tm, D_MODEL), F32)
    for j in range(D_FF // FF_COLS):
        gate = conv_cols(j * FF_COLS)
        up = conv_cols(D_FF + j * FF_COLS)
        act = (_gelu_tanh(gate) * up).astype(BF16)
        acc = acc + _dot(act, wdown_ref[j * FF_COLS:(j + 1) * FF_COLS, :])
    o_ref[0] = h + _rms(acc, gpost2_ref[...])


def _out_ffn(x, ya, yb, wout, gpost, gpre2, wup, fcw, fcb, wdown, gpost2):
    b, t, d = x.shape
    tm = TM_OUT
    tok = lambda n: pl.BlockSpec((1, tm, n), lambda i, j: (i, j, 0))
    full = lambda arr: pl.BlockSpec(arr.shape, lambda i, j: (0,) * arr.ndim)
    params = (wout, gpost, gpre2, wup, fcw, fcb, wdown, gpost2)
    return pl.pallas_call(
        _out_ffn_kernel,
        grid=(b, t // tm),
        in_specs=[tok(d), tok(CONV_DIM), tok(RWKV_DIM)] + [full(p) for p in params],
        out_specs=tok(d),
        out_shape=jax.ShapeDtypeStruct((b, t, d), F32),
        scratch_shapes=[pltpu.VMEM((SUBLANES, 2 * D_FF), F32)],
        compiler_params=pltpu.CompilerParams(
            dimension_semantics=("arbitrary", "arbitrary"), vmem_limit_bytes=VMEM_LIMIT),
        name="out_ffn",
    )(x, ya, yb, *params)


def _row(vec):
    return vec.reshape(1, -1).astype(F32)


def _layer(h, pre_mix_g, w_in, conv_a_w, shift_mu, w0, w2, a0, a2, g2, k_k, k_a, r_k, lnx_w, lnx_b,
           w_out, post_mix_g, pre_ffn_g, w_up, ffn_conv_w, ffn_conv_b, w_down, post_ffn_g):
    ca = 3 * CONV_DIM
    rb = ca + 3 * RWKV_DIM
    gate_pad = GATE_PAD - GATE_RANK
    wa = w_in[:, :ca].astype(BF16)
    wb = w_in[:, ca:rb].astype(BF16)
    wc = jnp.pad(w_in[:, rb:], ((0, 0), (0, gate_pad))).astype(BF16)
    mub = _row(shift_mu[:3 * RWKV_DIM])
    muc = _row(jnp.pad(shift_mu[3 * RWKV_DIM:], (0, gate_pad)))
    w2a2 = jnp.zeros((DECAY_RANK + ICLR_RANK, 2 * RWKV_DIM), F32)
    w2a2 = w2a2.at[:DECAY_RANK, :RWKV_DIM].set(w2).at[DECAY_RANK:, RWKV_DIM:].set(a2).astype(BF16)
    g2p = jnp.pad(g2, ((0, gate_pad), (0, 0))).astype(BF16)
    head = jnp.arange(RWKV_DIM) // HEAD_DIM
    hsum = (head[:, None] == head[None, :]).astype(F32)

    ya, r, k, v, kkn, a, lw, g = _mix_in(
        h, _row(pre_mix_g), wa, wb, wc, mub, muc, conv_a_w.astype(F32), _row(w0), _row(a0), w2a2,
        g2p, _row(k_k), _row(k_a), hsum)
    yb = _wkv(r, k, v, kkn, a, lw, g, _row(lnx_w), _row(lnx_b), _row(r_k))
    return _out_ffn(h, ya, yb, w_out.astype(BF16), _row(post_mix_g), _row(pre_ffn_g),
                    w_up.astype(BF16), ffn_conv_w.astype(F32), _row(ffn_conv_b),
                    w_down.astype(BF16), _row(post_ffn_g))


def kernel(x, pre_mix_g, w_in, conv_a_w, shift_mu, w0, w2, a0, a2, g2, k_k, k_a, r_k, lnx_w, lnx_b,
           w_out, post_mix_g, pre_ffn_g, w_up, ffn_conv_w, ffn_conv_b, w_down, post_ffn_g):
    h = x
    for l in range(pre_mix_g.shape[0]):
        h = _layer(h, pre_mix_g[l], w_in[l], conv_a_w[l], shift_mu[l], w0[l], w2[l], a0[l], a2[l],
                   g2[l], k_k[l], k_a[l], r_k[l], lnx_w[l], lnx_b[l], w_out[l], post_mix_g[l],
                   pre_ffn_g[l], w_up[l], ffn_conv_w[l], ffn_conv_b[l], w_down[l], post_ffn_g[l])
    return h
```

```python
import functools
import math

import jax
import jax.numpy as jnp
from jax import lax
from jax.experimental import pallas as pl
from jax.experimental.pallas import tpu as pltpu

D_MODEL = 1024
HEAD_DIM = 64
N_HEADS = 8
CONV_DIM = 512
RWKV_DIM = 512
DECAY_RANK = 64
ICLR_RANK = 64
GATE_RANK = 160
GATE_PAD = 256
D_FF = 2816
RMS_EPS = 1e-6
GN_EPS = HEAD_DIM * 1e-5

SUBLANES = 8
CHUNK = 64
TM_IN = 256
TT_WKV = 512
TM_OUT = 256
FF_COLS = 256
VMEM_LIMIT = 56 * 1024 * 1024

F32 = jnp.float32
BF16 = jnp.bfloat16
HIGHEST = lax.Precision.HIGHEST


def _dot(a, b):
    return jnp.dot(a, b, preferred_element_type=F32)


def _dot_nt(a, b):
    return lax.dot_general(a, b, (((1,), (1,)), ((), ())), preferred_element_type=F32)


def _dot_tn(a, b):
    return lax.dot_general(a, b, (((0,), (0,)), ((), ())), preferred_element_type=F32)


def _rms(z, g):
    return z * lax.rsqrt(jnp.mean(z * z, axis=-1, keepdims=True) + RMS_EPS) * g


def _sigmoid(z):
    return 1.0 / (1.0 + jnp.exp(-z))


def _shift_rows(z, prev, s):
    zs = pltpu.roll(z, s, 0)
    ps = pltpu.roll(prev, s, 0)
    row = lax.broadcasted_iota(jnp.int32, prev.shape, 0)
    head = jnp.where(row < s, ps, zs[:SUBLANES])
    return jnp.concatenate([head, zs[SUBLANES:]], axis=0)


def _mix_in_kernel(x_ref, gpre_ref, wa_ref, wb_ref, wc_ref, mub_ref, muc_ref, cw_ref, w0_ref,
                   a0_ref, w2a2_ref, g2_ref, kk_ref, ka_ref, hsum_ref,
                   ya_ref, r_ref, k_ref, v_ref, kkn_ref, a_ref, lw_ref, g_ref,
                   cu_ref, cqb_ref, cqc_ref):
    tm = x_ref.shape[1]

    @pl.when(pl.program_id(1) == 0)
    def _():
        cu_ref[...] = jnp.zeros_like(cu_ref)
        cqb_ref[...] = jnp.zeros_like(cqb_ref)
        cqc_ref[...] = jnp.zeros_like(cqc_ref)

    xn = _rms(x_ref[0], gpre_ref[...]).astype(BF16)

    pa = _dot(xn, wa_ref[...])
    u = pa[:, :CONV_DIM] * pa[:, 2 * CONV_DIM:]
    cu = cu_ref[...]
    cw = cw_ref[...]
    conv = (cw[0:1] * _shift_rows(u, cu, 2) + cw[1:2] * _shift_rows(u, cu, 1) + cw[2:3] * u)
    ya_ref[0] = pa[:, CONV_DIM:2 * CONV_DIM] * conv
    cu_ref[...] = u[tm - SUBLANES:]

    qb = _dot(xn, wb_ref[...])
    qbs = _shift_rows(qb, cqb_ref[...], 1)
    cqb_ref[...] = qb[tm - SUBLANES:]
    qb = qb + (qbs - qb) * mub_ref[...]
    qc = _dot(xn, wc_ref[...])
    qcs = _shift_rows(qc, cqc_ref[...], 1)
    cqc_ref[...] = qc[tm - SUBLANES:]
    qc = qc + (qcs - qc) * muc_ref[...]

    r = qb[:, :RWKV_DIM]
    k = qb[:, RWKV_DIM:2 * RWKV_DIM]
    v = qb[:, 2 * RWKV_DIM:]
    da = qc[:, :DECAY_RANK + ICLR_RANK]
    lane = lax.broadcasted_iota(jnp.int32, da.shape, 1)
    da = jnp.where(lane < DECAY_RANK, jnp.tanh(da), da)
    wa2 = _dot(da.astype(BF16), w2a2_ref[...])
    lw = -math.exp(-0.5) * _sigmoid(w0_ref[...] + wa2[:, :RWKV_DIM])
    a = _sigmoid(a0_ref[...] + wa2[:, RWKV_DIM:])
    g = _dot(_sigmoid(qc[:, DECAY_RANK + ICLR_RANK:]).astype(BF16), g2_ref[...])

    kk = k * kk_ref[...]
    ss = jnp.dot(kk * kk, hsum_ref[...], precision=HIGHEST, preferred_element_type=F32)
    kkn = kk * lax.rsqrt(jnp.maximum(ss, 1e-24))
    k = k * (1.0 + (a - 1.0) * ka_ref[...])

    r_ref[0] = r
    k_ref[0] = k
    v_ref[0] = v
    kkn_ref[0] = kkn
    a_ref[0] = a
    lw_ref[0] = lw
    g_ref[0] = g


def _mix_in(x, gpre, wa, wb, wc, mub, muc, cw, w0, a0, w2a2, g2p, k_k, k_a, hsum):
    b, t, d = x.shape
    tm = TM_IN
    grid = (b, t // tm)
    tok = lambda n: pl.BlockSpec((1, tm, n), lambda i, j: (i, j, 0))
    full = lambda arr: pl.BlockSpec(arr.shape, lambda i, j: (0,) * arr.ndim)
    params = (gpre, wa, wb, wc, mub, muc, cw, w0, a0, w2a2, g2p, k_k, k_a, hsum)
    out_sds = jax.ShapeDtypeStruct((b, t, RWKV_DIM), F32)
    return pl.pallas_call(
        _mix_in_kernel,
        grid=grid,
        in_specs=[tok(d)] + [full(p) for p in params],
        out_specs=[tok(RWKV_DIM)] * 8,
        out_shape=[out_sds] * 8,
        scratch_shapes=[pltpu.VMEM((SUBLANES, CONV_DIM), F32),
                        pltpu.VMEM((SUBLANES, 3 * RWKV_DIM), F32),
                        pltpu.VMEM((SUBLANES, wc.shape[1]), F32)],
        compiler_params=pltpu.CompilerParams(
            dimension_semantics=("arbitrary", "arbitrary"), vmem_limit_bytes=VMEM_LIMIT),
        name="mix_in",
    )(x, *params)


def _split_bf16(x, parts):
    out = []
    for _ in range(parts - 1):
        hi = x.astype(BF16)
        out.append(hi)
        x = x - hi.astype(F32)
    out.append(x.astype(BF16))
    return out


def _wkv_kernel(r_ref, k_ref, v_ref, kk_ref, a_ref, lw_ref, g_ref, lnw_ref, lnb_ref, rk_ref,
                hones_ref, yb_ref,
                s_ref, rt_s, vb_s, w_s, rb_s, rk_s, bkt_s, ut_s, y_s, pcol_s):
    tt = r_ref.shape[1]
    c = CHUNK
    blk = 2 * c
    pair = 2 * HEAD_DIM
    n_pairs = N_HEADS // 2
    quad = 2 * pair

    @pl.when(pl.program_id(1) == 0)
    def _():
        s_ref[...] = jnp.zeros_like(s_ref)

    def iota(shape, dim):
        return lax.broadcasted_iota(jnp.int32, shape, dim)

    tri2 = ((iota((blk, blk), 0) >= iota((blk, blk), 1))
            & ((iota((blk, blk), 0) // c) == (iota((blk, blk), 1) // c))).astype(BF16)
    rowc = iota((c, pair), 0)
    colc = iota((c, pair), 1) % c
    incl2 = rowc >= colc
    strict2 = rowc > colc
    lane_lo = iota((c, pair), 1) < HEAD_DIM
    eye4 = (iota((c, quad), 0) == (iota((c, quad), 1) % c)).astype(F32)
    bd4 = (iota((quad, quad), 0) // c) == (iota((quad, quad), 1) // c)
    bd2 = (iota((pair, pair), 0) // HEAD_DIM) == (iota((pair, pair), 1) // HEAD_DIM)
    zero_b = jnp.zeros((c, pair), BF16)

    def head_stack(x):
        return jnp.concatenate([jnp.where(lane_lo, x, zero_b), jnp.where(lane_lo, zero_b, x)], axis=0)

    def block_diag4(p4):
        pb = p4.astype(BF16)
        return jnp.where(bd4, jnp.concatenate([pb] * 4, axis=0), jnp.zeros((quad, quad), BF16))

    def phase1(i, carry):
        base = pl.multiple_of(i * blk, blk)
        rows = pl.ds(base, blk)
        r = r_ref[0, rows, :]
        k = k_ref[0, rows, :]
        kk = kk_ref[0, rows, :]
        lw = lw_ref[0, rows, :]
        vb = v_ref[0, rows, :].astype(BF16)
        beta = kk * a_ref[0, rows, :]

        lcum = sum(_dot(tri2, part) for part in _split_bf16(lw, 3))
        l_end0 = lcum[c - 1:c, :]
        l_end1 = lcum[blk - 1:blk, :]
        lend = jnp.where(iota(lcum.shape, 0) < c, l_end0, l_end1)
        e_neg = jnp.exp(-lcum)
        e_end = jnp.exp(lend - lcum)
        r_t = (r * jnp.exp(lcum)).astype(BF16)
        a_t = (-kk * jnp.exp(lcum - lw)).astype(BF16)
        k_t = (k * e_neg).astype(BF16)
        b_t = (beta * e_neg).astype(BF16)
        b_e = beta * e_end
        k_e = k * e_end
        rt_s[rows, :] = r_t
        vb_s[rows, :] = vb

        units = [(q, p) for q in range(2) for p in range(n_pairs)]
        p_end = [jnp.exp(l_end0), jnp.exp(l_end1)]
        sub = iota((SUBLANES, pair), 0)
        p_rows = jnp.zeros((SUBLANES, pair), F32)
        for j, (q, p) in enumerate(units):
            p_rows = jnp.where(sub == j, p_end[q][:, p * pair:(p + 1) * pair], p_rows)
        p_cols = jnp.concatenate([p_rows, jnp.zeros((pair - SUBLANES, pair), F32)], axis=0).T
        for j, (q, p) in enumerate(units):
            pcol_s[2 * i + q, :, p * pair:(p + 1) * pair] = jnp.broadcast_to(
                p_cols[:, j:j + 1], (pair, pair))
        ab, ak = {}, {}
        for q, p in units:
            rs = slice(q * c, (q + 1) * c)
            ls = slice(p * pair, (p + 1) * pair)
            lhs = jnp.concatenate([a_t[rs, ls], r_t[rs, ls]], axis=0)
            rhs = jnp.concatenate([head_stack(b_t[rs, ls]), head_stack(k_t[rs, ls])], axis=0)
            out = _dot_nt(lhs, rhs)
            ab[q, p] = jnp.where(strict2, out[:c, :pair], 0.0)
            ak[q, p] = jnp.where(strict2, out[:c, pair:], 0.0).astype(BF16)
            qrows = pl.ds(base + q * c, c)
            rb_s[qrows, ls] = jnp.where(incl2, out[c:, :pair], 0.0).astype(BF16)
            rk_s[qrows, ls] = jnp.where(incl2, out[c:, pair:], 0.0).astype(BF16)
            bk = jnp.concatenate([b_e[rs, ls], k_e[rs, ls]], axis=0)
            bkt_s[pl.ds(2 * (base + q * c), 2 * c), ls] = bk.T.astype(BF16)

        quads = [(q, hf) for q in range(2) for hf in range(n_pairs // 2)]
        pw = {u: jnp.concatenate([ab[u[0], 2 * u[1]], ab[u[0], 2 * u[1] + 1]], axis=1) for u in quads}
        tinv = {u: eye4 + pw[u] for u in quads}
        pw = {u: _dot(pw[u].astype(BF16), block_diag4(pw[u])) for u in quads}
        for _ in range(int(math.log2(c)) - 2):
            for u in quads:
                out = _dot(jnp.concatenate([pw[u].astype(BF16), tinv[u].astype(BF16)], axis=0),
                           block_diag4(pw[u]))
                pw[u] = out[:c]
                tinv[u] = tinv[u] + out[c:]
        tinv = {u: (tinv[u] + _dot(tinv[u].astype(BF16), block_diag4(pw[u]))).astype(BF16)
                for u in quads}

        aakv = {}
        for q, p in units:
            rs = slice(q * c, (q + 1) * c)
            ls = slice(p * pair, (p + 1) * pair)
            aakv[q, p] = _dot(ak[q, p], head_stack(vb[rs, ls])).astype(BF16)
        for q, p in units:
            rs = slice(q * c, (q + 1) * c)
            ls = slice(p * pair, (p + 1) * pair)
            t2 = tinv[q, p // 2][:, (p % 2) * pair:(p % 2 + 1) * pair]
            rhs = jnp.concatenate([head_stack(a_t[rs, ls]), head_stack(aakv[q, p])], axis=1)
            wu = _dot(t2, rhs)
            qrows = pl.ds(base + q * c, c)
            w_s[qrows, ls] = wu[:, :pair].astype(BF16)
            ut_s[qrows, ls] = wu[:, pair:]
        return carry

    lax.fori_loop(0, tt // blk, phase1, 0)

    def phase2(ci, carry):
        rows = pl.ds(pl.multiple_of(ci * c, c), c)
        krows = pl.ds(pl.multiple_of(ci * 2 * c, 2 * c), 2 * c)
        lanes = [slice(p * pair, (p + 1) * pair) for p in range(n_pairs)]
        z = [s_ref[p] for p in range(n_pairs)]
        wr = [_dot(jnp.concatenate([w_s[rows, ls], rt_s[rows, ls]], axis=0), z[p].astype(BF16))
              for p, ls in enumerate(lanes)]
        ub = [(wr[p][:c] + ut_s[rows, ls]).astype(BF16) for p, ls in enumerate(lanes)]
        vb = [vb_s[rows, ls] for ls in lanes]
        upd = [_dot(bkt_s[krows, ls], jnp.concatenate([ub[p], vb[p]], axis=0))
               for p, ls in enumerate(lanes)]
        for p, ls in enumerate(lanes):
            s_ref[p] = z[p] * pcol_s[ci, :, ls] + jnp.where(bd2, upd[p], 0.0)
        for p, ls in enumerate(lanes):
            y_s[rows, ls] = wr[p][c:] + _dot(
                jnp.concatenate([rb_s[rows, ls], rk_s[rows, ls]], axis=1),
                jnp.concatenate([head_stack(ub[p]), head_stack(vb[p])], axis=0))
        return carry

    lax.fori_loop(0, tt // c, phase2, 0)

    hones = hones_ref[...]
    half = hones.shape[0]

    def head_sum(x):
        parts = _split_bf16(x, 2)
        return jnp.concatenate(
            [sum(_dot(part[:, j * half:(j + 1) * half], hones) for part in parts)
             for j in range(x.shape[1] // half)], axis=1)

    def phase3(i, carry):
        rows = pl.ds(pl.multiple_of(i * blk, blk), blk)
        y = y_s[rows, :]
        v = v_ref[0, rows, :]
        mu = head_sum(y) * (1.0 / HEAD_DIM)
        yc = y - mu
        var = head_sum(yc * yc) * (1.0 / HEAD_DIM)
        yn = yc * lax.rsqrt(var + GN_EPS) * lnw_ref[...] + lnb_ref[...]
        bonus = head_sum(r_ref[0, rows, :] * k_ref[0, rows, :] * rk_ref[...]) * v
        yb_ref[0, rows, :] = (yn + bonus) * g_ref[0, rows, :]
        return carry

    lax.fori_loop(0, tt // blk, phase3, 0)


def _wkv(r, k, v, kkn, a, lw, g, lnw, lnb, rk):
    b, t, n = r.shape
    tt = TT_WKV
    tok = pl.BlockSpec((1, tt, n), lambda i, j: (i, j, 0))
    vec = pl.BlockSpec((1, n), lambda i, j: (0, 0))
    half = n // 2
    head = jnp.arange(half) // HEAD_DIM
    hones = (head[:, None] == head[None, :]).astype(BF16)
    bf_tile = pltpu.VMEM((tt, n), BF16)
    f32_tile = pltpu.VMEM((tt, n), F32)
    return pl.pallas_call(
        _wkv_kernel,
        grid=(b, t // tt),
        in_specs=[tok] * 7 + [vec] * 3 + [pl.BlockSpec((half, half), lambda i, j: (0, 0))],
        out_specs=tok,
        out_shape=jax.ShapeDtypeStruct((b, t, n), F32),
        scratch_shapes=[pltpu.VMEM((N_HEADS // 2, 2 * HEAD_DIM, 2 * HEAD_DIM), F32)]
        + [bf_tile] * 5 + [pltpu.VMEM((2 * tt, n), BF16)] + [f32_tile] * 2
        + [pltpu.VMEM((tt // CHUNK, 2 * HEAD_DIM, n), F32)],
        compiler_params=pltpu.CompilerParams(
            dimension_semantics=("arbitrary", "arbitrary"), vmem_limit_bytes=VMEM_LIMIT),
        name="wkv",
    )(r, k, v, kkn, a, lw, g, lnw, lnb, rk, hones)


def _gelu_tanh(z):
    return 0.5 * z * (1.0 + jnp.tanh(math.sqrt(2.0 / math.pi) * (z + 0.044715 * (z * z * z))))


def _out_ffn_kernel(x_ref, ya_ref, yb_ref, wout_ref, gpost_ref, gpre2_ref, wup_ref, fcw_ref,
                    fcb_ref, wdown_ref, gpost2_ref, o_ref, cf_ref):
    tm = x_ref.shape[1]

    @pl.when(pl.program_id(1) == 0)
    def _():
        cf_ref[...] = jnp.zeros_like(cf_ref)

    ycat = jnp.concatenate([ya_ref[0], yb_ref[0]], axis=1).astype(BF16)
    mix = _dot(ycat, wout_ref[...])
    h = x_ref[0] + _rms(mix, gpost_ref[...])
    hn = _rms(h, gpre2_ref[...]).astype(BF16)

    def conv_cols(lo):
        cols = slice(lo, lo + FF_COLS)
        f = _dot(hn, wup_ref[:, cols])
        prev = cf_ref[:, cols]
        cw = fcw_ref[:, cols]
        out = (cw[0:1] * _shift_rows(f, prev, 2) + cw[1:2] * _shift_rows(f, prev, 1)
               + cw[2:3] * f + fcb_ref[:, cols])
        cf_ref[:, cols] = f[tm - SUBLANES:]
        return out

    acc = jnp.zeros((tm, D_MODEL), F32)
    for j in range(D_FF // FF_COLS):
        gate = conv_cols(j * FF_COLS)
        up = conv_cols(D_FF + j * FF_COLS)
        act = (_gelu_tanh(gate) * up).astype(BF16)
        acc = acc + _dot(act, wdown_ref[j * FF_COLS:(j + 1) * FF_COLS, :])
    o_ref[0] = h + _rms(acc, gpost2_ref[...])


def _out_ffn(x, ya, yb, wout, gpost, gpre2, wup, fcw, fcb, wdown, gpost2):
    b, t, d = x.shape
    tm = TM_OUT
    tok = lambda n: pl.BlockSpec((1, tm, n), lambda i, j: (i, j, 0))
    full = lambda arr: pl.BlockSpec(arr.shape, lambda i, j: (0,) * arr.ndim)
    params = (wout, gpost, gpre2, wup, fcw, fcb, wdown, gpost2)
    return pl.pallas_call(
        _out_ffn_kernel,
        grid=(b, t // tm),
        in_specs=[tok(d), tok(CONV_DIM), tok(RWKV_DIM)] + [full(p) for p in params],
        out_specs=tok(d),
        out_shape=jax.ShapeDtypeStruct((b, t, d), F32),
        scratch_shapes=[pltpu.VMEM((SUBLANES, 2 * D_FF), F32)],
        compiler_params=pltpu.CompilerParams(
            dimension_semantics=("arbitrary", "arbitrary"), vmem_limit_bytes=VMEM_LIMIT),
        name="out_ffn",
    )(x, ya, yb, *params)


def _row(vec):
    return vec.reshape(1, -1).astype(F32)


def _layer(h, pre_mix_g, w_in, conv_a_w, shift_mu, w0, w2, a0, a2, g2, k_k, k_a, r_k, lnx_w, lnx_b,
           w_out, post_mix_g, pre_ffn_g, w_up, ffn_conv_w, ffn_conv_b, w_down, post_ffn_g):
    ca = 3 * CONV_DIM
    rb = ca + 3 * RWKV_DIM
    gate_pad = GATE_PAD - GATE_RANK
    wa = w_in[:, :ca].astype(BF16)
    wb = w_in[:, ca:rb].astype(BF16)
    wc = jnp.pad(w_in[:, rb:], ((0, 0), (0, gate_pad))).astype(BF16)
    mub = _row(shift_mu[:3 * RWKV_DIM])
    muc = _row(jnp.pad(shift_mu[3 * RWKV_DIM:], (0, gate_pad)))
    w2a2 = jnp.zeros((DECAY_RANK + ICLR_RANK, 2 * RWKV_DIM), F32)
    w2a2 = w2a2.at[:DECAY_RANK, :RWKV_DIM].set(w2).at[DECAY_RANK:, RWKV_DIM:].set(a2).astype(BF16)
    g2p = jnp.pad(g2, ((0, gate_pad), (0, 0))).astype(BF16)
    head = jnp.arange(RWKV_DIM) // HEAD_DIM
    hsum = (head[:, None] == head[None, :]).astype(F32)

    ya, r, k, v, kkn, a, lw, g = _mix_in(
        h, _row(pre_mix_g), wa, wb, wc, mub, muc, conv_a_w.astype(F32), _row(w0), _row(a0), w2a2,
        g2p, _row(k_k), _row(k_a), hsum)
    yb = _wkv(r, k, v, kkn, a, lw, g, _row(lnx_w), _row(lnx_b), _row(r_k))
    return _out_ffn(h, ya, yb, w_out.astype(BF16), _row(post_mix_g), _row(pre_ffn_g),
                    w_up.astype(BF16), ffn_conv_w.astype(F32), _row(ffn_conv_b),
                    w_down.astype(BF16), _row(post_ffn_g))


def kernel(x, pre_mix_g, w_in, conv_a_w, shift_mu, w0, w2, a0, a2, g2, k_k, k_a, r_k, lnx_w, lnx_b,
           w_out, post_mix_g, pre_ffn_g, w_up, ffn_conv_w, ffn_conv_b, w_down, post_ffn_g):
    h = x
    for l in range(pre_mix_g.shape[0]):
        h = _layer(h, pre_mix_g[l], w_in[l], conv_a_w[l], shift_mu[l], w0[l], w2[l], a0[l], a2[l],
                   g2[l], k_k[l], k_a[l], r_k[l], lnx_w[l], lnx_b[l], w_out[l], post_mix_g[l],
                   pre_ffn_g[l], w_up[l], ffn_conv_w[l], ffn_conv_b[l], w_down[l], post_ffn_g[l])
    return h
```

```python
import functools
import math

import jax
import jax.numpy as jnp
from jax import lax
from jax.experimental import pallas as pl
from jax.experimental.pallas import tpu as pltpu

D_MODEL = 1024
HEAD_DIM = 64
N_HEADS = 8
CONV_DIM = 512
RWKV_DIM = 512
DECAY_RANK = 64
ICLR_RANK = 64
GATE_RANK = 160
GATE_PAD = 256
D_FF = 2816
RMS_EPS = 1e-6
GN_EPS = HEAD_DIM * 1e-5

SUBLANES = 8
CHUNK = 64
TM_IN = 512
TT_WKV = 512
TM_OUT = 512
TM_SUB = 256
FF_COLS = 256
HEAD_ONES = 256
VMEM_LIMIT = 56 * 1024 * 1024

F32 = jnp.float32
BF16 = jnp.bfloat16
HIGHEST = lax.Precision.HIGHEST


def _dot(a, b):
    return jnp.dot(a, b, preferred_element_type=F32)


def _dot_nt(a, b):
    return lax.dot_general(a, b, (((1,), (1,)), ((), ())), preferred_element_type=F32)


def _dot_tn(a, b):
    return lax.dot_general(a, b, (((0,), (0,)), ((), ())), preferred_element_type=F32)


def _rms(z, g):
    return z * lax.rsqrt(jnp.mean(z * z, axis=-1, keepdims=True) + RMS_EPS) * g


def _sigmoid(z):
    return 1.0 / (1.0 + jnp.exp(-z))


def _head_sum(xb, hones):
    m = hones.shape[0]
    return jnp.concatenate([_dot(xb[:, j * m:(j + 1) * m], hones) for j in range(xb.shape[1] // m)],
                           axis=1)


def _shift_rows(z, prev, s):
    zs = pltpu.roll(z, s, 0)
    ps = pltpu.roll(prev, s, 0)
    row = lax.broadcasted_iota(jnp.int32, prev.shape, 0)
    head = jnp.where(row < s, ps, zs[:SUBLANES])
    return jnp.concatenate([head, zs[SUBLANES:]], axis=0)


def _mix_in_kernel(x_ref, gpre_ref, wa_ref, wb_ref, wc_ref, mub_ref, muc_ref, cw_ref, w0_ref,
                   a0_ref, w2a2_ref, g2_ref, kk_ref, ka_ref, hsum_ref,
                   ya_ref, r_ref, k_ref, v_ref, kkn_ref, a_ref, lw_ref, g_ref,
                   cu_ref, cqb_ref, cqc_ref):
    tm = x_ref.shape[1]
    sub = TM_SUB

    @pl.when(pl.program_id(1) == 0)
    def _():
        cu_ref[...] = jnp.zeros_like(cu_ref)
        cqb_ref[...] = jnp.zeros_like(cqb_ref)
        cqc_ref[...] = jnp.zeros_like(cqc_ref)

    subs = [pl.ds(i * sub, sub) for i in range(tm // sub)]
    proj = []
    for rs in subs:
        xn = _rms(x_ref[0, rs, :], gpre_ref[...]).astype(BF16)
        proj.append((_dot(xn, wa_ref[...]), _dot(xn, wb_ref[...]), _dot(xn, wc_ref[...])))

    for rs, (pa, qb, qc) in zip(subs, proj):
        u = pa[:, :CONV_DIM] * pa[:, 2 * CONV_DIM:]
        cu = cu_ref[...]
        cw = cw_ref[...]
        conv = (cw[0:1] * _shift_rows(u, cu, 2) + cw[1:2] * _shift_rows(u, cu, 1) + cw[2:3] * u)
        ya_ref[0, rs, :] = (pa[:, CONV_DIM:2 * CONV_DIM] * conv).astype(ya_ref.dtype)
        cu_ref[...] = u[sub - SUBLANES:]

        qbs = _shift_rows(qb, cqb_ref[...], 1)
        cqb_ref[...] = qb[sub - SUBLANES:]
        qb = qb + (qbs - qb) * mub_ref[...]
        qcs = _shift_rows(qc, cqc_ref[...], 1)
        cqc_ref[...] = qc[sub - SUBLANES:]
        qc = qc + (qcs - qc) * muc_ref[...]

        r = qb[:, :RWKV_DIM]
        k = qb[:, RWKV_DIM:2 * RWKV_DIM]
        v = qb[:, 2 * RWKV_DIM:]
        da = qc[:, :DECAY_RANK + ICLR_RANK]
        lane = lax.broadcasted_iota(jnp.int32, da.shape, 1)
        da = jnp.where(lane < DECAY_RANK, jnp.tanh(da), da)
        wa2 = _dot(da.astype(BF16), w2a2_ref[...])
        lw = -math.exp(-0.5) * _sigmoid(w0_ref[...] + wa2[:, :RWKV_DIM])
        a = _sigmoid(a0_ref[...] + wa2[:, RWKV_DIM:])
        g = _dot(_sigmoid(qc[:, DECAY_RANK + ICLR_RANK:]).astype(BF16), g2_ref[...])

        kk = k * kk_ref[...]
        ss = _head_sum((kk * kk).astype(BF16), hsum_ref[...])
        kkn = kk * lax.rsqrt(jnp.maximum(ss, 1e-24))
        k = k * (1.0 + (a - 1.0) * ka_ref[...])

        r_ref[0, rs, :] = r
        k_ref[0, rs, :] = k
        v_ref[0, rs, :] = v
        kkn_ref[0, rs, :] = kkn
        a_ref[0, rs, :] = a
        lw_ref[0, rs, :] = lw
        g_ref[0, rs, :] = g


def _mix_in(x, gpre, wa, wb, wc, mub, muc, cw, w0, a0, w2a2, g2p, k_k, k_a, hsum):
    b, t, d = x.shape
    tm = TM_IN
    grid = (b, t // tm)
    tok = lambda n: pl.BlockSpec((1, tm, n), lambda i, j: (i, j, 0))
    full = lambda arr: pl.BlockSpec(arr.shape, lambda i, j: (0,) * arr.ndim)
    params = (gpre, wa, wb, wc, mub, muc, cw, w0, a0, w2a2, g2p, k_k, k_a, hsum)
    out_sds = jax.ShapeDtypeStruct((b, t, RWKV_DIM), F32)
    return pl.pallas_call(
        _mix_in_kernel,
        grid=grid,
        in_specs=[tok(d)] + [full(p) for p in params],
        out_specs=[tok(RWKV_DIM)] * 8,
        out_shape=[jax.ShapeDtypeStruct((b, t, CONV_DIM), BF16)] + [out_sds] * 7,
        scratch_shapes=[pltpu.VMEM((SUBLANES, CONV_DIM), F32),
                        pltpu.VMEM((SUBLANES, 3 * RWKV_DIM), F32),
                        pltpu.VMEM((SUBLANES, wc.shape[1]), F32)],
        compiler_params=pltpu.CompilerParams(
            dimension_semantics=("arbitrary", "arbitrary"), vmem_limit_bytes=VMEM_LIMIT),
        name="mix_in",
    )(x, *params)


def _split_bf16(x, parts):
    out = []
    for _ in range(parts - 1):
        hi = x.astype(BF16)
        out.append(hi)
        x = x - hi.astype(F32)
    out.append(x.astype(BF16))
    return out


def _wkv_kernel(r_ref, k_ref, v_ref, kk_ref, a_ref, lw_ref, g_ref, lnw_ref, lnb_ref, rk_ref,
                hones_ref, yb_ref,
                s_ref, rt_s, vb_s, w_s, rb_s, rk_s, bkt_s, ut_s, y_s, pcol_s):
    tt = r_ref.shape[1]
    c = CHUNK
    blk = 2 * c
    pair = 2 * HEAD_DIM
    n_pairs = N_HEADS // 2
    quad = 2 * pair

    @pl.when(pl.program_id(1) == 0)
    def _():
        s_ref[...] = jnp.zeros_like(s_ref)

    def iota(shape, dim):
        return lax.broadcasted_iota(jnp.int32, shape, dim)

    tri2 = ((iota((blk, blk), 0) >= iota((blk, blk), 1))
            & ((iota((blk, blk), 0) // c) == (iota((blk, blk), 1) // c))).astype(BF16)
    rowc = iota((c, pair), 0)
    colc = iota((c, pair), 1) % c
    incl2 = rowc >= colc
    strict2 = rowc > colc
    lane_lo = iota((c, pair), 1) < HEAD_DIM
    eye4 = (iota((c, quad), 0) == (iota((c, quad), 1) % c)).astype(F32)
    bd4 = (iota((quad, quad), 0) // c) == (iota((quad, quad), 1) // c)
    bd2 = (iota((pair, pair), 0) // HEAD_DIM) == (iota((pair, pair), 1) // HEAD_DIM)
    zero_b = jnp.zeros((c, pair), BF16)

    def head_stack(x):
        return jnp.concatenate([jnp.where(lane_lo, x, zero_b), jnp.where(lane_lo, zero_b, x)], axis=0)

    def block_diag4(p4):
        pb = p4.astype(BF16)
        return jnp.where(bd4, jnp.concatenate([pb] * 4, axis=0), jnp.zeros((quad, quad), BF16))

    def phase1(i, carry):
        base = pl.multiple_of(i * blk, blk)
        rows = pl.ds(base, blk)
        r = r_ref[0, rows, :]
        k = k_ref[0, rows, :]
        kk = kk_ref[0, rows, :]
        lw = lw_ref[0, rows, :]
        vb = v_ref[0, rows, :].astype(BF16)
        beta = kk * a_ref[0, rows, :]

        lcum = sum(_dot(tri2, part) for part in _split_bf16(lw, 2))
        l_end0 = lcum[c - 1:c, :]
        l_end1 = lcum[blk - 1:blk, :]
        lend = jnp.where(iota(lcum.shape, 0) < c, l_end0, l_end1)
        e_neg = jnp.exp(-lcum)
        e_end = jnp.exp(lend - lcum)
        r_t = (r * jnp.exp(lcum)).astype(BF16)
        a_t = (-kk * jnp.exp(lcum - lw)).astype(BF16)
        k_t = (k * e_neg).astype(BF16)
        b_t = (beta * e_neg).astype(BF16)
        b_e = beta * e_end
        k_e = k * e_end
        rt_s[rows, :] = r_t
        vb_s[rows, :] = vb

        units = [(q, p) for q in range(2) for p in range(n_pairs)]
        p_end = [jnp.exp(l_end0), jnp.exp(l_end1)]
        sub = iota((SUBLANES, pair), 0)
        p_rows = jnp.zeros((SUBLANES, pair), F32)
        for j, (q, p) in enumerate(units):
            p_rows = jnp.where(sub == j, p_end[q][:, p * pair:(p + 1) * pair], p_rows)
        p_cols = jnp.concatenate([p_rows, jnp.zeros((pair - SUBLANES, pair), F32)], axis=0).T
        for j, (q, p) in enumerate(units):
            pcol_s[2 * i + q, :, p * pair:(p + 1) * pair] = jnp.broadcast_to(
                p_cols[:, j:j + 1], (pair, pair))
        ab, ak = {}, {}
        for q, p in units:
            rs = slice(q * c, (q + 1) * c)
            ls = slice(p * pair, (p + 1) * pair)
            lhs = jnp.concatenate([a_t[rs, ls], r_t[rs, ls]], axis=0)
            rhs = jnp.concatenate([head_stack(b_t[rs, ls]), head_stack(k_t[rs, ls])], axis=0)
            out = _dot_nt(lhs, rhs)
            ab[q, p] = jnp.where(strict2, out[:c, :pair], 0.0)
            ak[q, p] = jnp.where(strict2, out[:c, pair:], 0.0).astype(BF16)
            qrows = pl.ds(base + q * c, c)
            rb_s[qrows, ls] = jnp.where(incl2, out[c:, :pair], 0.0).astype(BF16)
            rk_s[qrows, ls] = jnp.where(incl2, out[c:, pair:], 0.0).astype(BF16)
            bk = jnp.concatenate([b_e[rs, ls], k_e[rs, ls]], axis=0)
            bkt_s[pl.ds(2 * (base + q * c), 2 * c), ls] = bk.T.astype(BF16)

        quads = [(q, hf) for q in range(2) for hf in range(n_pairs // 2)]
        pw = {u: jnp.concatenate([ab[u[0], 2 * u[1]], ab[u[0], 2 * u[1] + 1]], axis=1) for u in quads}
        tinv = {u: eye4 + pw[u] for u in quads}
        pw = {u: _dot(pw[u].astype(BF16), block_diag4(pw[u])) for u in quads}
        for _ in range(int(math.log2(c)) - 2):
            for u in quads:
                out = _dot(jnp.concatenate([pw[u].astype(BF16), tinv[u].astype(BF16)], axis=0),
                           block_diag4(pw[u]))
                pw[u] = out[:c]
                tinv[u] = tinv[u] + out[c:]
        tinv = {u: (tinv[u] + _dot(tinv[u].astype(BF16), block_diag4(pw[u]))).astype(BF16)
                for u in quads}

        aakv = {}
        for q, p in units:
            rs = slice(q * c, (q + 1) * c)
            ls = slice(p * pair, (p + 1) * pair)
            aakv[q, p] = _dot(ak[q, p], head_stack(vb[rs, ls])).astype(BF16)
        for q, p in units:
            rs = slice(q * c, (q + 1) * c)
            ls = slice(p * pair, (p + 1) * pair)
            t2 = tinv[q, p // 2][:, (p % 2) * pair:(p % 2 + 1) * pair]
            rhs = jnp.concatenate([head_stack(a_t[rs, ls]), head_stack(aakv[q, p])], axis=1)
            wu = _dot(t2, rhs)
            qrows = pl.ds(base + q * c, c)
            w_s[qrows, ls] = wu[:, :pair].astype(BF16)
            ut_s[qrows, ls] = wu[:, pair:]
        return carry

    lax.fori_loop(0, tt // blk, phase1, 0)

    def phase2(ci, carry):
        rows = pl.ds(pl.multiple_of(ci * c, c), c)
        krows = pl.ds(pl.multiple_of(ci * 2 * c, 2 * c), 2 * c)
        lanes = [slice(p * pair, (p + 1) * pair) for p in range(n_pairs)]
        z = [s_ref[p] for p in range(n_pairs)]
        wr = [_dot(jnp.concatenate([w_s[rows, ls], rt_s[rows, ls]], axis=0), z[p].astype(BF16))
              for p, ls in enumerate(lanes)]
        ub = [(wr[p][:c] + ut_s[rows, ls]).astype(BF16) for p, ls in enumerate(lanes)]
        vb = [vb_s[rows, ls] for ls in lanes]
        upd = [_dot(bkt_s[krows, ls], jnp.concatenate([ub[p], vb[p]], axis=0))
               for p, ls in enumerate(lanes)]
        for p, ls in enumerate(lanes):
            s_ref[p] = z[p] * pcol_s[ci, :, ls] + jnp.where(bd2, upd[p], 0.0)
        for p, ls in enumerate(lanes):
            y_s[rows, ls] = wr[p][c:] + _dot(
                jnp.concatenate([rb_s[rows, ls], rk_s[rows, ls]], axis=1),
                jnp.concatenate([head_stack(ub[p]), head_stack(vb[p])], axis=0))
        return carry

    lax.fori_loop(0, tt // c, phase2, 0)

    hones = hones_ref[...]

    def head_sum(x):
        return _head_sum(x.astype(BF16), hones)

    def phase3(i, carry):
        rows = pl.ds(pl.multiple_of(i * blk, blk), blk)
        y = y_s[rows, :]
        v = v_ref[0, rows, :]
        mu = head_sum(y) * (1.0 / HEAD_DIM)
        yc = y - mu
        var = head_sum(yc * yc) * (1.0 / HEAD_DIM)
        yn = yc * lax.rsqrt(var + GN_EPS) * lnw_ref[...] + lnb_ref[...]
        bonus = head_sum(r_ref[0, rows, :] * k_ref[0, rows, :] * rk_ref[...]) * v
        yb_ref[0, rows, :] = ((yn + bonus) * g_ref[0, rows, :]).astype(yb_ref.dtype)
        return carry

    lax.fori_loop(0, tt // blk, phase3, 0)


def _wkv(r, k, v, kkn, a, lw, g, lnw, lnb, rk):
    b, t, n = r.shape
    tt = TT_WKV
    tok = pl.BlockSpec((1, tt, n), lambda i, j: (i, j, 0))
    vec = pl.BlockSpec((1, n), lambda i, j: (0, 0))
    hones = _head_ones()
    bf_tile = pltpu.VMEM((tt, n), BF16)
    f32_tile = pltpu.VMEM((tt, n), F32)
    return pl.pallas_call(
        _wkv_kernel,
        grid=(b, t // tt),
        in_specs=[tok] * 7 + [vec] * 3 + [pl.BlockSpec(hones.shape, lambda i, j: (0, 0))],
        out_specs=tok,
        out_shape=jax.ShapeDtypeStruct((b, t, n), BF16),
        scratch_shapes=[pltpu.VMEM((N_HEADS // 2, 2 * HEAD_DIM, 2 * HEAD_DIM), F32)]
        + [bf_tile] * 5 + [pltpu.VMEM((2 * tt, n), BF16)] + [f32_tile] * 2
        + [pltpu.VMEM((tt // CHUNK, 2 * HEAD_DIM, n), F32)],
        compiler_params=pltpu.CompilerParams(
            dimension_semantics=("arbitrary", "arbitrary"), vmem_limit_bytes=VMEM_LIMIT),
        name="wkv",
    )(r, k, v, kkn, a, lw, g, lnw, lnb, rk, hones)


def _gelu_tanh(z):
    return 0.5 * z * (1.0 + jnp.tanh(math.sqrt(2.0 / math.pi) * (z + 0.044715 * (z * z * z))))


def _out_ffn_kernel(x_ref, ya_ref, yb_ref, wout_ref, gpost_ref, gpre2_ref, wup_ref, fcw_ref,
                    fcb_ref, wdown_ref, gpost2_ref, o_ref, cf_ref):
    tm = x_ref.shape[1]
    sub = TM_SUB

    @pl.when(pl.program_id(1) == 0)
    def _():
        cf_ref[...] = jnp.zeros_like(cf_ref)

    subs = [pl.ds(i * sub, sub) for i in range(tm // sub)]
    mixes = [_dot(jnp.concatenate([ya_ref[0, rs, :], yb_ref[0, rs, :]], axis=1), wout_ref[...])
             for rs in subs]

    def conv_cols(f, lo):
        cols = slice(lo, lo + FF_COLS)
        prev = cf_ref[:, cols]
        cw = fcw_ref[:, cols]
        out = (cw[0:1] * _shift_rows(f, prev, 2) + cw[1:2] * _shift_rows(f, prev, 1)
               + cw[2:3] * f + fcb_ref[:, cols])
        cf_ref[:, cols] = f[sub - SUBLANES:]
        return out

    n_groups = D_FF // FF_COLS
    for rs, mix in zip(subs, mixes):
        h = x_ref[0, rs, :] + _rms(mix, gpost_ref[...])
        hn = _rms(h, gpre2_ref[...]).astype(BF16)

        def up_proj(j, hn=hn):
            return [_dot(hn, wup_ref[:, lo:lo + FF_COLS]) for lo in (j * FF_COLS, D_FF + j * FF_COLS)]

        acc = None
        nxt = up_proj(0)
        for j in range(n_groups):
            f_gate, f_up = nxt
            if j + 1 < n_groups:
                nxt = up_proj(j + 1)
            gate = conv_cols(f_gate, j * FF_COLS)
            up = conv_cols(f_up, D_FF + j * FF_COLS)
            act = (_gelu_tanh(gate) * up).astype(BF16)
            part = _dot(act, wdown_ref[j * FF_COLS:(j + 1) * FF_COLS, :])
            acc = part if acc is None else acc + part
        o_ref[0, rs, :] = h + _rms(acc, gpost2_ref[...])


def _out_ffn(x, ya, yb, wout, gpost, gpre2, wup, fcw, fcb, wdown, gpost2):
    b, t, d = x.shape
    tm = TM_OUT
    tok = lambda n: pl.BlockSpec((1, tm, n), lambda i, j: (i, j, 0))
    full = lambda arr: pl.BlockSpec(arr.shape, lambda i, j: (0,) * arr.ndim)
    params = (wout, gpost, gpre2, wup, fcw, fcb, wdown, gpost2)
    return pl.pallas_call(
        _out_ffn_kernel,
        grid=(b, t // tm),
        in_specs=[tok(d), tok(CONV_DIM), tok(RWKV_DIM)] + [full(p) for p in params],
        out_specs=tok(d),
        out_shape=jax.ShapeDtypeStruct((b, t, d), F32),
        scratch_shapes=[pltpu.VMEM((SUBLANES, 2 * D_FF), F32)],
        compiler_params=pltpu.CompilerParams(
            dimension_semantics=("arbitrary", "arbitrary"), vmem_limit_bytes=VMEM_LIMIT),
        name="out_ffn",
    )(x, ya, yb, *params)


def _row(vec):
    return vec.reshape(1, -1).astype(F32)


def _head_ones():
    head = jnp.arange(HEAD_ONES) // HEAD_DIM
    return (head[:, None] == head[None, :]).astype(BF16)


def _layer(h, pre_mix_g, w_in, conv_a_w, shift_mu, w0, w2, a0, a2, g2, k_k, k_a, r_k, lnx_w, lnx_b,
           w_out, post_mix_g, pre_ffn_g, w_up, ffn_conv_w, ffn_conv_b, w_down, post_ffn_g):
    ca = 3 * CONV_DIM
    rb = ca + 3 * RWKV_DIM
    gate_pad = GATE_PAD - GATE_RANK
    wa = w_in[:, :ca].astype(BF16)
    wb = w_in[:, ca:rb].astype(BF16)
    wc = jnp.pad(w_in[:, rb:], ((0, 0), (0, gate_pad))).astype(BF16)
    mub = _row(shift_mu[:3 * RWKV_DIM])
    muc = _row(jnp.pad(shift_mu[3 * RWKV_DIM:], (0, gate_pad)))
    w2a2 = jnp.zeros((DECAY_RANK + ICLR_RANK, 2 * RWKV_DIM), F32)
    w2a2 = w2a2.at[:DECAY_RANK, :RWKV_DIM].set(w2).at[DECAY_RANK:, RWKV_DIM:].set(a2).astype(BF16)
    g2p = jnp.pad(g2, ((0, gate_pad), (0, 0))).astype(BF16)
    hsum = _head_ones()

    ya, r, k, v, kkn, a, lw, g = _mix_in(
        h, _row(pre_mix_g), wa, wb, wc, mub, muc, conv_a_w.astype(F32), _row(w0), _row(a0), w2a2,
        g2p, _row(k_k), _row(k_a), hsum)
    yb = _wkv(r, k, v, kkn, a, lw, g, _row(lnx_w), _row(lnx_b), _row(r_k))
    return _out_ffn(h, ya, yb, w_out.astype(BF16), _row(post_mix_g), _row(pre_ffn_g),
                    w_up.astype(BF16), ffn_conv_w.astype(F32), _row(ffn_conv_b),
                    w_down.astype(BF16), _row(post_ffn_g))


def kernel(x, pre_mix_g, w_in, conv_a_w, shift_mu, w0, w2, a0, a2, g2, k_k, k_a, r_k, lnx_w, lnx_b,
           w_out, post_mix_g, pre_ffn_g, w_up, ffn_conv_w, ffn_conv_b, w_down, post_ffn_g):
    h = x
    for l in range(pre_mix_g.shape[0]):
        h = _layer(h, pre_mix_g[l], w_in[l], conv_a_w[l], shift_mu[l], w0[l], w2[l], a0[l], a2[l],
                   g2[l], k_k[l], k_a[l], r_k[l], lnx_w[l], lnx_b[l], w_out[l], post_mix_g[l],
                   pre_ffn_g[l], w_up[l], ffn_conv_w[l], ffn_conv_b[l], w_down[l], post_ffn_g[l])
    return h
```

```python
import functools
import math

import jax
import jax.numpy as jnp
from jax import lax
from jax.experimental import pallas as pl
from jax.experimental.pallas import tpu as pltpu

D_MODEL = 1024
HEAD_DIM = 64
N_HEADS = 8
CONV_DIM = 512
RWKV_DIM = 512
DECAY_RANK = 64
ICLR_RANK = 64
GATE_RANK = 160
GATE_PAD = 256
D_FF = 2816
RMS_EPS = 1e-6
GN_EPS = HEAD_DIM * 1e-5

SUBLANES = 8
CHUNK = 64
WKV_BLOCK_CHUNKS = 4
TM_IN = 512
TT_WKV = 512
TM_OUT = 512
TM_SUB = 256
FF_COLS = 256
HEAD_ONES = 256
VMEM_LIMIT = 56 * 1024 * 1024

F32 = jnp.float32
BF16 = jnp.bfloat16
HIGHEST = lax.Precision.HIGHEST


def _dot(a, b):
    return jnp.dot(a, b, preferred_element_type=F32)


def _dot_nt(a, b):
    return lax.dot_general(a, b, (((1,), (1,)), ((), ())), preferred_element_type=F32)


def _dot_tn(a, b):
    return lax.dot_general(a, b, (((0,), (0,)), ((), ())), preferred_element_type=F32)


def _rms(z, g):
    return z * lax.rsqrt(jnp.mean(z * z, axis=-1, keepdims=True) + RMS_EPS) * g


def _sigmoid(z):
    return 1.0 / (1.0 + jnp.exp(-z))


def _head_sum(xb, hones):
    m = hones.shape[0]
    return jnp.concatenate([_dot(xb[:, j * m:(j + 1) * m], hones) for j in range(xb.shape[1] // m)],
                           axis=1)


def _shift_rows(z, prev, s):
    zs = pltpu.roll(z, s, 0)
    ps = pltpu.roll(prev, s, 0)
    row = lax.broadcasted_iota(jnp.int32, prev.shape, 0)
    head = jnp.where(row < s, ps, zs[:SUBLANES])
    return jnp.concatenate([head, zs[SUBLANES:]], axis=0)


def _mix_in_kernel(x_ref, gpre_ref, wa_ref, wb_ref, wc_ref, mub_ref, muc_ref, cw_ref, w0_ref,
                   a0_ref, w2a2_ref, g2_ref, kk_ref, ka_ref, hsum_ref,
                   ya_ref, r_ref, k_ref, v_ref, kkn_ref, a_ref, lw_ref, g_ref,
                   cu_ref, cqb_ref, cqc_ref):
    tm = x_ref.shape[1]
    sub = TM_SUB

    @pl.when(pl.program_id(1) == 0)
    def _():
        cu_ref[...] = jnp.zeros_like(cu_ref)
        cqb_ref[...] = jnp.zeros_like(cqb_ref)
        cqc_ref[...] = jnp.zeros_like(cqc_ref)

    subs = [pl.ds(i * sub, sub) for i in range(tm // sub)]
    proj = []
    for rs in subs:
        xn = _rms(x_ref[0, rs, :], gpre_ref[...]).astype(BF16)
        proj.append((_dot(xn, wa_ref[...]), _dot(xn, wb_ref[...]), _dot(xn, wc_ref[...])))

    for rs, (pa, qb, qc) in zip(subs, proj):
        u = pa[:, :CONV_DIM] * pa[:, 2 * CONV_DIM:]
        cu = cu_ref[...]
        cw = cw_ref[...]
        conv = (cw[0:1] * _shift_rows(u, cu, 2) + cw[1:2] * _shift_rows(u, cu, 1) + cw[2:3] * u)
        ya_ref[0, rs, :] = (pa[:, CONV_DIM:2 * CONV_DIM] * conv).astype(ya_ref.dtype)
        cu_ref[...] = u[sub - SUBLANES:]

        qbs = _shift_rows(qb, cqb_ref[...], 1)
        cqb_ref[...] = qb[sub - SUBLANES:]
        qb = qb + (qbs - qb) * mub_ref[...]
        qcs = _shift_rows(qc, cqc_ref[...], 1)
        cqc_ref[...] = qc[sub - SUBLANES:]
        qc = qc + (qcs - qc) * muc_ref[...]

        r = qb[:, :RWKV_DIM]
        k = qb[:, RWKV_DIM:2 * RWKV_DIM]
        v = qb[:, 2 * RWKV_DIM:]
        da = qc[:, :DECAY_RANK + ICLR_RANK]
        lane = lax.broadcasted_iota(jnp.int32, da.shape, 1)
        da = jnp.where(lane < DECAY_RANK, jnp.tanh(da), da)
        wa2 = _dot(da.astype(BF16), w2a2_ref[...])
        lw = -math.exp(-0.5) * _sigmoid(w0_ref[...] + wa2[:, :RWKV_DIM])
        a = _sigmoid(a0_ref[...] + wa2[:, RWKV_DIM:])
        g = _dot(_sigmoid(qc[:, DECAY_RANK + ICLR_RANK:]).astype(BF16), g2_ref[...])

        kk = k * kk_ref[...]
        ss = _head_sum((kk * kk).astype(BF16), hsum_ref[...])
        kkn = kk * lax.rsqrt(jnp.maximum(ss, 1e-24))
        k = k * (1.0 + (a - 1.0) * ka_ref[...])

        r_ref[0, rs, :] = r
        k_ref[0, rs, :] = k
        v_ref[0, rs, :] = v
        kkn_ref[0, rs, :] = kkn
        a_ref[0, rs, :] = a
        lw_ref[0, rs, :] = lw
        g_ref[0, rs, :] = g


def _mix_in(x, gpre, wa, wb, wc, mub, muc, cw, w0, a0, w2a2, g2p, k_k, k_a, hsum):
    b, t, d = x.shape
    tm = TM_IN
    grid = (b, t // tm)
    tok = lambda n: pl.BlockSpec((1, tm, n), lambda i, j: (i, j, 0))
    full = lambda arr: pl.BlockSpec(arr.shape, lambda i, j: (0,) * arr.ndim)
    params = (gpre, wa, wb, wc, mub, muc, cw, w0, a0, w2a2, g2p, k_k, k_a, hsum)
    out_sds = jax.ShapeDtypeStruct((b, t, RWKV_DIM), F32)
    return pl.pallas_call(
        _mix_in_kernel,
        grid=grid,
        in_specs=[tok(d)] + [full(p) for p in params],
        out_specs=[tok(RWKV_DIM)] * 8,
        out_shape=[jax.ShapeDtypeStruct((b, t, CONV_DIM), BF16)] + [out_sds] * 7,
        scratch_shapes=[pltpu.VMEM((SUBLANES, CONV_DIM), F32),
                        pltpu.VMEM((SUBLANES, 3 * RWKV_DIM), F32),
                        pltpu.VMEM((SUBLANES, wc.shape[1]), F32)],
        compiler_params=pltpu.CompilerParams(
            dimension_semantics=("arbitrary", "arbitrary"), vmem_limit_bytes=VMEM_LIMIT),
        name="mix_in",
    )(x, *params)


def _split_bf16(x, parts):
    out = []
    for _ in range(parts - 1):
        hi = x.astype(BF16)
        out.append(hi)
        x = x - hi.astype(F32)
    out.append(x.astype(BF16))
    return out


def _interleave(*stages):
    live = list(stages)
    while live:
        still = []
        for gen in live:
            try:
                next(gen)
                still.append(gen)
            except StopIteration:
                pass
        live = still


def _wkv_kernel(r_ref, k_ref, v_ref, kk_ref, a_ref, lw_ref, g_ref, lnw_ref, lnb_ref, rk_ref,
                hones_ref, yb_ref,
                s_ref, rt_s, vb_s, w_s, rb_s, rk_s, gs_s, bg_s, bkt_s, ut_s, pcol_s, y_s):
    tt = r_ref.shape[1]
    c = CHUNK
    nq = WKV_BLOCK_CHUNKS
    blk = nq * c
    pair = 2 * HEAD_DIM
    n_pairs = N_HEADS // 2
    quad = 2 * pair
    lanes = [slice(p * pair, (p + 1) * pair) for p in range(n_pairs)]
    step = pl.program_id(1)
    wslot = step % 2
    rslot = 1 - wslot
    slotted = (rt_s, vb_s, w_s, rb_s, rk_s, gs_s, bg_s, bkt_s, ut_s, pcol_s)

    @pl.when(step == 0)
    def _():
        s_ref[...] = jnp.zeros_like(s_ref)
        for ref in slotted:
            ref[1] = jnp.zeros(ref.shape[1:], ref.dtype)

    def iota(shape, dim):
        return lax.broadcasted_iota(jnp.int32, shape, dim)

    tri2 = ((iota((blk, blk), 0) >= iota((blk, blk), 1))
            & ((iota((blk, blk), 0) // c) == (iota((blk, blk), 1) // c))).astype(BF16)
    rowc = iota((c, pair), 0)
    colc = iota((c, pair), 1) % c
    incl2 = rowc >= colc
    strict2 = rowc > colc
    lane_lo = iota((c, pair), 1) < HEAD_DIM
    eye4 = (iota((c, quad), 0) == (iota((c, quad), 1) % c)).astype(F32)
    bd4 = (iota((quad, quad), 0) // c) == (iota((quad, quad), 1) // c)
    bd2 = (iota((pair, pair), 0) // HEAD_DIM) == (iota((pair, pair), 1) // HEAD_DIM)
    zero_b = jnp.zeros((c, pair), BF16)
    hones = hones_ref[...]

    def head_stack(x):
        return jnp.concatenate([jnp.where(lane_lo, x, zero_b), jnp.where(lane_lo, zero_b, x)], axis=0)

    def block_diag4(p4):
        pb = p4.astype(BF16)
        return jnp.where(bd4, jnp.concatenate([pb] * 4, axis=0), jnp.zeros((quad, quad), BF16))

    def head_sum(x):
        return _head_sum(x.astype(BF16), hones)

    def phase1(bi):
        base = bi * blk
        rows = slice(base, base + blk)
        r = r_ref[0, rows, :]
        k = k_ref[0, rows, :]
        v = v_ref[0, rows, :]
        kk = kk_ref[0, rows, :]
        lw = lw_ref[0, rows, :]
        g = g_ref[0, rows, :]
        vb = v.astype(BF16)
        beta = kk * a_ref[0, rows, :]
        lcum = sum(_dot(tri2, part) for part in _split_bf16(lw, 2))
        bonus = head_sum(r * k * rk_ref[...]) * v
        yield

        l_end = [lcum[(q + 1) * c - 1:(q + 1) * c, :] for q in range(nq)]
        chunk_of_row = iota(lcum.shape, 0) // c
        lend = l_end[0]
        for q in range(1, nq):
            lend = jnp.where(chunk_of_row == q, l_end[q], lend)
        e_neg = jnp.exp(-lcum)
        e_end = jnp.exp(lend - lcum)
        r_t = (r * jnp.exp(lcum)).astype(BF16)
        a_t = (-kk * jnp.exp(lcum - lw)).astype(BF16)
        k_t = (k * e_neg).astype(BF16)
        b_t = (beta * e_neg).astype(BF16)
        b_e = beta * e_end
        k_e = k * e_end
        rt_s[wslot, rows, :] = r_t
        vb_s[wslot, rows, :] = vb
        gs_s[wslot, rows, :] = g.astype(BF16)
        bg_s[wslot, rows, :] = (bonus * g).astype(BF16)

        units = [(q, p) for q in range(nq) for p in range(n_pairs)]
        p_end = [jnp.exp(le) for le in l_end]
        n_rows = -(-len(units) // SUBLANES) * SUBLANES
        sub = iota((n_rows, pair), 0)
        p_rows = jnp.zeros((n_rows, pair), F32)
        for j, (q, p) in enumerate(units):
            p_rows = jnp.where(sub == j, p_end[q][:, lanes[p]], p_rows)
        p_cols = jnp.concatenate([p_rows, jnp.zeros((pair - n_rows, pair), F32)], axis=0).T
        for j, (q, p) in enumerate(units):
            pcol_s[wslot, bi * nq + q, :, lanes[p]] = jnp.broadcast_to(p_cols[:, j:j + 1], (pair, pair))

        ab, ak = {}, {}
        for q, p in units:
            rs = slice(q * c, (q + 1) * c)
            ls = lanes[p]
            lhs = jnp.concatenate([a_t[rs, ls], r_t[rs, ls]], axis=0)
            rhs = jnp.concatenate([head_stack(b_t[rs, ls]), head_stack(k_t[rs, ls])], axis=0)
            out = _dot_nt(lhs, rhs)
            ab[q, p] = jnp.where(strict2, out[:c, :pair], 0.0)
            ak[q, p] = jnp.where(strict2, out[:c, pair:], 0.0).astype(BF16)
            qrows = slice(base + q * c, base + (q + 1) * c)
            rb_s[wslot, qrows, ls] = jnp.where(incl2, out[c:, :pair], 0.0).astype(BF16)
            rk_s[wslot, qrows, ls] = jnp.where(incl2, out[c:, pair:], 0.0).astype(BF16)
            bk = jnp.concatenate([b_e[rs, ls], k_e[rs, ls]], axis=0)
            bkt_s[wslot, 2 * base + 2 * q * c:2 * base + 2 * (q + 1) * c, ls] = bk.T.astype(BF16)
        yield

        quads = [(q, hf) for q in range(nq) for hf in range(n_pairs // 2)]
        pw = {u: jnp.concatenate([ab[u[0], 2 * u[1]], ab[u[0], 2 * u[1] + 1]], axis=1) for u in quads}
        tinv = {u: eye4 + pw[u] for u in quads}
        pw = {u: _dot(pw[u].astype(BF16), block_diag4(pw[u])) for u in quads}
        yield
        for _ in range(int(math.log2(c)) - 2):
            for u in quads:
                out = _dot(jnp.concatenate([pw[u].astype(BF16), tinv[u].astype(BF16)], axis=0),
                           block_diag4(pw[u]))
                pw[u] = out[:c]
                tinv[u] = tinv[u] + out[c:]
            yield
        tinv = {u: (tinv[u] + _dot(tinv[u].astype(BF16), block_diag4(pw[u]))).astype(BF16)
                for u in quads}
        yield

        aakv = {}
        for q, p in units:
            rs = slice(q * c, (q + 1) * c)
            aakv[q, p] = _dot(ak[q, p], head_stack(vb[rs, lanes[p]])).astype(BF16)
        yield
        for q, p in units:
            rs = slice(q * c, (q + 1) * c)
            ls = lanes[p]
            t2 = tinv[q, p // 2][:, (p % 2) * pair:(p % 2 + 1) * pair]
            rhs = jnp.concatenate([head_stack(a_t[rs, ls]), head_stack(aakv[q, p])], axis=1)
            wu = _dot(t2, rhs)
            qrows = slice(base + q * c, base + (q + 1) * c)
            w_s[wslot, qrows, ls] = wu[:, :pair].astype(BF16)
            ut_s[wslot, qrows, ls] = wu[:, pair:]

    def phase2(bi, z):
        pending = None
        for q in range(nq):
            ci = bi * nq + q
            rows = slice(ci * c, (ci + 1) * c)
            krows = slice(2 * ci * c, 2 * (ci + 1) * c)
            wr = [_dot(jnp.concatenate([w_s[rslot, rows, ls], rt_s[rslot, rows, ls]], axis=0),
                       z[p].astype(BF16)) for p, ls in enumerate(lanes)]
            if pending is not None:
                pending()
            yield
            ub = [(wr[p][:c] + ut_s[rslot, rows, ls]).astype(BF16) for p, ls in enumerate(lanes)]
            vb = [vb_s[rslot, rows, ls] for ls in lanes]
            upd = [_dot(bkt_s[rslot, krows, ls], jnp.concatenate([ub[p], vb[p]], axis=0))
                   for p, ls in enumerate(lanes)]
            for p, ls in enumerate(lanes):
                z[p] = z[p] * pcol_s[rslot, ci, :, ls] + jnp.where(bd2, upd[p], 0.0)

            def emit_y(rows=rows, wr=wr, ub=ub, vb=vb):
                for p, ls in enumerate(lanes):
                    y_s[rows, ls] = wr[p][c:] + _dot(
                        jnp.concatenate([rb_s[rslot, rows, ls], rk_s[rslot, rows, ls]], axis=1),
                        jnp.concatenate([head_stack(ub[p]), head_stack(vb[p])], axis=0))
            pending = emit_y
            yield
        pending()

    def phase3(rows):
        y = y_s[rows, :]
        mu = head_sum(y) * (1.0 / HEAD_DIM)
        yc = y - mu
        var = head_sum(yc * yc) * (1.0 / HEAD_DIM)
        yn = yc * lax.rsqrt(var + GN_EPS) * lnw_ref[...] + lnb_ref[...]
        yb_ref[0, rows, :] = (yn * gs_s[rslot, rows, :] + bg_s[rslot, rows, :]).astype(yb_ref.dtype)

    z = [s_ref[p] for p in range(n_pairs)]
    for bi in range(tt // blk):
        _interleave(phase1(bi), phase2(bi, z))
        for half in range(blk // (2 * c)):
            phase3(slice(bi * blk + half * 2 * c, bi * blk + (half + 1) * 2 * c))
    for p in range(n_pairs):
        s_ref[p] = z[p]


def _wkv(r, k, v, kkn, a, lw, g, lnw, lnb, rk):
    b, t, n = r.shape
    tt = TT_WKV
    nt = t // tt
    tok_in = pl.BlockSpec((1, tt, n), lambda i, j: (i, jnp.minimum(j, nt - 1), 0))
    tok_out = pl.BlockSpec((1, tt, n), lambda i, j: (i, jnp.maximum(j - 1, 0), 0))
    vec = pl.BlockSpec((1, n), lambda i, j: (0, 0))
    hones = _head_ones()
    bf_tile = pltpu.VMEM((2, tt, n), BF16)
    return pl.pallas_call(
        _wkv_kernel,
        grid=(b, nt + 1),
        in_specs=[tok_in] * 7 + [vec] * 3 + [pl.BlockSpec(hones.shape, lambda i, j: (0, 0))],
        out_specs=tok_out,
        out_shape=jax.ShapeDtypeStruct((b, t, n), BF16),
        scratch_shapes=[pltpu.VMEM((N_HEADS // 2, 2 * HEAD_DIM, 2 * HEAD_DIM), F32)]
        + [bf_tile] * 7 + [pltpu.VMEM((2, 2 * tt, n), BF16), pltpu.VMEM((2, tt, n), F32),
                           pltpu.VMEM((2, tt // CHUNK, 2 * HEAD_DIM, n), F32),
                           pltpu.VMEM((tt, n), F32)],
        compiler_params=pltpu.CompilerParams(
            dimension_semantics=("arbitrary", "arbitrary"), vmem_limit_bytes=VMEM_LIMIT),
        name="wkv",
    )(r, k, v, kkn, a, lw, g, lnw, lnb, rk, hones)


def _gelu_tanh(z):
    return 0.5 * z * (1.0 + jnp.tanh(math.sqrt(2.0 / math.pi) * (z + 0.044715 * (z * z * z))))


def _out_ffn_kernel(x_ref, ya_ref, yb_ref, wout_ref, gpost_ref, gpre2_ref, wup_ref, fcw_ref,
                    fcb_ref, wdown_ref, gpost2_ref, o_ref, cf_ref):
    tm = x_ref.shape[1]
    sub = TM_SUB

    @pl.when(pl.program_id(1) == 0)
    def _():
        cf_ref[...] = jnp.zeros_like(cf_ref)

    subs = [pl.ds(i * sub, sub) for i in range(tm // sub)]
    mixes = [_dot(jnp.concatenate([ya_ref[0, rs, :], yb_ref[0, rs, :]], axis=1), wout_ref[...])
             for rs in subs]

    def conv_cols(f, lo):
        cols = slice(lo, lo + FF_COLS)
        prev = cf_ref[:, cols]
        cw = fcw_ref[:, cols]
        out = (cw[0:1] * _shift_rows(f, prev, 2) + cw[1:2] * _shift_rows(f, prev, 1)
               + cw[2:3] * f + fcb_ref[:, cols])
        cf_ref[:, cols] = f[sub - SUBLANES:]
        return out

    n_groups = D_FF // FF_COLS
    for rs, mix in zip(subs, mixes):
        h = x_ref[0, rs, :] + _rms(mix, gpost_ref[...])
        hn = _rms(h, gpre2_ref[...]).astype(BF16)

        def up_proj(j, hn=hn):
            return [_dot(hn, wup_ref[:, lo:lo + FF_COLS]) for lo in (j * FF_COLS, D_FF + j * FF_COLS)]

        acc = None
        nxt = up_proj(0)
        for j in range(n_groups):
            f_gate, f_up = nxt
            if j + 1 < n_groups:
                nxt = up_proj(j + 1)
            gate = conv_cols(f_gate, j * FF_COLS)
            up = conv_cols(f_up, D_FF + j * FF_COLS)
            act = (_gelu_tanh(gate) * up).astype(BF16)
            part = _dot(act, wdown_ref[j * FF_COLS:(j + 1) * FF_COLS, :])
            acc = part if acc is None else acc + part
        o_ref[0, rs, :] = h + _rms(acc, gpost2_ref[...])


def _out_ffn(x, ya, yb, wout, gpost, gpre2, wup, fcw, fcb, wdown, gpost2):
    b, t, d = x.shape
    tm = TM_OUT
    tok = lambda n: pl.BlockSpec((1, tm, n), lambda i, j: (i, j, 0))
    full = lambda arr: pl.BlockSpec(arr.shape, lambda i, j: (0,) * arr.ndim)
    params = (wout, gpost, gpre2, wup, fcw, fcb, wdown, gpost2)
    return pl.pallas_call(
        _out_ffn_kernel,
        grid=(b, t // tm),
        in_specs=[tok(d), tok(CONV_DIM), tok(RWKV_DIM)] + [full(p) for p in params],
        out_specs=tok(d),
        out_shape=jax.ShapeDtypeStruct((b, t, d), F32),
        scratch_shapes=[pltpu.VMEM((SUBLANES, 2 * D_FF), F32)],
        compiler_params=pltpu.CompilerParams(
            dimension_semantics=("arbitrary", "arbitrary"), vmem_limit_bytes=VMEM_LIMIT),
        name="out_ffn",
    )(x, ya, yb, *params)


def _row(vec):
    return vec.reshape(1, -1).astype(F32)


def _head_ones():
    head = jnp.arange(HEAD_ONES) // HEAD_DIM
    return (head[:, None] == head[None, :]).astype(BF16)


def _layer(h, pre_mix_g, w_in, conv_a_w, shift_mu, w0, w2, a0, a2, g2, k_k, k_a, r_k, lnx_w, lnx_b,
           w_out, post_mix_g, pre_ffn_g, w_up, ffn_conv_w, ffn_conv_b, w_down, post_ffn_g):
    ca = 3 * CONV_DIM
    rb = ca + 3 * RWKV_DIM
    gate_pad = GATE_PAD - GATE_RANK
    wa = w_in[:, :ca].astype(BF16)
    wb = w_in[:, ca:rb].astype(BF16)
    wc = jnp.pad(w_in[:, rb:], ((0, 0), (0, gate_pad))).astype(BF16)
    mub = _row(shift_mu[:3 * RWKV_DIM])
    muc = _row(jnp.pad(shift_mu[3 * RWKV_DIM:], (0, gate_pad)))
    w2a2 = jnp.zeros((DECAY_RANK + ICLR_RANK, 2 * RWKV_DIM), F32)
    w2a2 = w2a2.at[:DECAY_RANK, :RWKV_DIM].set(w2).at[DECAY_RANK:, RWKV_DIM:].set(a2).astype(BF16)
    g2p = jnp.pad(g2, ((0, gate_pad), (0, 0))).astype(BF16)
    hsum = _head_ones()

    ya, r, k, v, kkn, a, lw, g = _mix_in(
        h, _row(pre_mix_g), wa, wb, wc, mub, muc, conv_a_w.astype(F32), _row(w0), _row(a0), w2a2,
        g2p, _row(k_k), _row(k_a), hsum)
    yb = _wkv(r, k, v, kkn, a, lw, g, _row(lnx_w), _row(lnx_b), _row(r_k))
    return _out_ffn(h, ya, yb, w_out.astype(BF16), _row(post_mix_g), _row(pre_ffn_g),
                    w_up.astype(BF16), ffn_conv_w.astype(F32), _row(ffn_conv_b),
                    w_down.astype(BF16), _row(post_ffn_g))


def kernel(x, pre_mix_g, w_in, conv_a_w, shift_mu, w0, w2, a0, a2, g2, k_k, k_a, r_k, lnx_w, lnx_b,
           w_out, post_mix_g, pre_ffn_g, w_up, ffn_conv_w, ffn_conv_b, w_down, post_ffn_g):
    h = x
    for l in range(pre_mix_g.shape[0]):
        h = _layer(h, pre_mix_g[l], w_in[l], conv_a_w[l], shift_mu[l], w0[l], w2[l], a0[l], a2[l],
                   g2[l], k_k[l], k_a[l], r_k[l], lnx_w[l], lnx_b[l], w_out[l], post_mix_g[l],
                   pre_ffn_g[l], w_up[l], ffn_conv_w[l], ffn_conv_b[l], w_down[l], post_ffn_g[l])
    return h
```

```python
import functools
import math

import jax
import jax.numpy as jnp
from jax import lax
from jax.experimental import pallas as pl
from jax.experimental.pallas import tpu as pltpu

D_MODEL = 1024
HEAD_DIM = 64
N_HEADS = 8
CONV_DIM = 512
RWKV_DIM = 512
DECAY_RANK = 64
ICLR_RANK = 64
GATE_RANK = 160
GATE_PAD = 256
D_FF = 2816
RMS_EPS = 1e-6
GN_EPS = HEAD_DIM * 1e-5

SUBLANES = 8
CHUNK = 64
WKV_BLOCK_CHUNKS = 4
TM_IN = 512
TT_WKV = 512
TM_OUT = 512
TM_SUB = 256
FF_COLS = 256
HEAD_ONES = 256
VMEM_LIMIT = 56 * 1024 * 1024

F32 = jnp.float32
BF16 = jnp.bfloat16
HIGHEST = lax.Precision.HIGHEST


def _dot(a, b):
    return jnp.dot(a, b, preferred_element_type=F32)


def _dot_nt(a, b):
    return lax.dot_general(a, b, (((1,), (1,)), ((), ())), preferred_element_type=F32)


def _dot_tn(a, b):
    return lax.dot_general(a, b, (((0,), (0,)), ((), ())), preferred_element_type=F32)


def _rms(z, g):
    return z * lax.rsqrt(jnp.mean(z * z, axis=-1, keepdims=True) + RMS_EPS) * g


def _sigmoid(z):
    return 1.0 / (1.0 + jnp.exp(-z))


def _head_sum(xb, hones):
    m = hones.shape[0]
    rows = xb.shape[0]
    slabs = xb.shape[1] // m
    out = _dot(jnp.concatenate([xb[:, j * m:(j + 1) * m] for j in range(slabs)], axis=0), hones)
    return jnp.concatenate([out[j * rows:(j + 1) * rows] for j in range(slabs)], axis=1)


def _shift_rows(z, prev, s):
    zs = pltpu.roll(z, s, 0)
    ps = pltpu.roll(prev, s, 0)
    row = lax.broadcasted_iota(jnp.int32, prev.shape, 0)
    head = jnp.where(row < s, ps, zs[:SUBLANES])
    return jnp.concatenate([head, zs[SUBLANES:]], axis=0)


def _mix_in_kernel(x_ref, gpre_ref, wa_ref, wb_ref, wc_ref, mub_ref, muc_ref, cw_ref, w0_ref,
                   a0_ref, w2a2_ref, g2_ref, kk_ref, ka_ref, hsum_ref,
                   ya_ref, r_ref, k_ref, v_ref, kkn_ref, a_ref, lw_ref, g_ref,
                   cu_ref, cqb_ref, cqc_ref):
    tm = x_ref.shape[1]
    sub = TM_SUB

    @pl.when(pl.program_id(1) == 0)
    def _():
        cu_ref[...] = jnp.zeros_like(cu_ref)
        cqb_ref[...] = jnp.zeros_like(cqb_ref)
        cqc_ref[...] = jnp.zeros_like(cqc_ref)

    subs = [pl.ds(i * sub, sub) for i in range(tm // sub)]
    proj = []
    for rs in subs:
        xn = _rms(x_ref[0, rs, :], gpre_ref[...]).astype(BF16)
        proj.append((_dot(xn, wa_ref[...]), _dot(xn, wb_ref[...]), _dot(xn, wc_ref[...])))

    for rs, (pa, qb, qc) in zip(subs, proj):
        u = pa[:, :CONV_DIM] * pa[:, 2 * CONV_DIM:]
        cu = cu_ref[...]
        cw = cw_ref[...]
        conv = (cw[0:1] * _shift_rows(u, cu, 2) + cw[1:2] * _shift_rows(u, cu, 1) + cw[2:3] * u)
        ya_ref[0, rs, :] = (pa[:, CONV_DIM:2 * CONV_DIM] * conv).astype(ya_ref.dtype)
        cu_ref[...] = u[sub - SUBLANES:]

        qbs = _shift_rows(qb, cqb_ref[...], 1)
        cqb_ref[...] = qb[sub - SUBLANES:]
        qb = qb + (qbs - qb) * mub_ref[...]
        qcs = _shift_rows(qc, cqc_ref[...], 1)
        cqc_ref[...] = qc[sub - SUBLANES:]
        qc = qc + (qcs - qc) * muc_ref[...]

        r = qb[:, :RWKV_DIM]
        k = qb[:, RWKV_DIM:2 * RWKV_DIM]
        v = qb[:, 2 * RWKV_DIM:]
        da = qc[:, :DECAY_RANK + ICLR_RANK]
        lane = lax.broadcasted_iota(jnp.int32, da.shape, 1)
        da = jnp.where(lane < DECAY_RANK, jnp.tanh(da), da)
        wa2 = _dot(da.astype(BF16), w2a2_ref[...])
        lw = -math.exp(-0.5) * _sigmoid(w0_ref[...] + wa2[:, :RWKV_DIM])
        a = _sigmoid(a0_ref[...] + wa2[:, RWKV_DIM:])
        g = _dot(_sigmoid(qc[:, DECAY_RANK + ICLR_RANK:]).astype(BF16), g2_ref[...])

        kk = k * kk_ref[...]
        ss = _head_sum((kk * kk).astype(BF16), hsum_ref[...])
        kkn = kk * lax.rsqrt(jnp.maximum(ss, 1e-24))
        k = k * (1.0 + (a - 1.0) * ka_ref[...])

        r_ref[0, rs, :] = r
        k_ref[0, rs, :] = k
        v_ref[0, rs, :] = v
        kkn_ref[0, rs, :] = kkn
        a_ref[0, rs, :] = a
        lw_ref[0, rs, :] = lw
        g_ref[0, rs, :] = g


def _mix_in(x, gpre, wa, wb, wc, mub, muc, cw, w0, a0, w2a2, g2p, k_k, k_a, hsum):
    b, t, d = x.shape
    tm = TM_IN
    grid = (b, t // tm)
    tok = lambda n: pl.BlockSpec((1, tm, n), lambda i, j: (i, j, 0))
    full = lambda arr: pl.BlockSpec(arr.shape, lambda i, j: (0,) * arr.ndim)
    params = (gpre, wa, wb, wc, mub, muc, cw, w0, a0, w2a2, g2p, k_k, k_a, hsum)
    out_sds = jax.ShapeDtypeStruct((b, t, RWKV_DIM), F32)
    return pl.pallas_call(
        _mix_in_kernel,
        grid=grid,
        in_specs=[tok(d)] + [full(p) for p in params],
        out_specs=[tok(RWKV_DIM)] * 8,
        out_shape=[jax.ShapeDtypeStruct((b, t, CONV_DIM), BF16)] + [out_sds] * 7,
        scratch_shapes=[pltpu.VMEM((SUBLANES, CONV_DIM), F32),
                        pltpu.VMEM((SUBLANES, 3 * RWKV_DIM), F32),
                        pltpu.VMEM((SUBLANES, wc.shape[1]), F32)],
        compiler_params=pltpu.CompilerParams(
            dimension_semantics=("arbitrary", "arbitrary"), vmem_limit_bytes=VMEM_LIMIT),
        name="mix_in",
    )(x, *params)


def _split_bf16(x, parts):
    out = []
    for _ in range(parts - 1):
        hi = x.astype(BF16)
        out.append(hi)
        x = x - hi.astype(F32)
    out.append(x.astype(BF16))
    return out


def _interleave(*stages):
    live = list(stages)
    while live:
        still = []
        for gen in live:
            try:
                next(gen)
                still.append(gen)
            except StopIteration:
                pass
        live = still


def _wkv_kernel(r_ref, k_ref, v_ref, kk_ref, a_ref, lw_ref, g_ref, lnw_ref, lnb_ref, rk_ref,
                hones_ref, yb_ref,
                s_ref, rt_s, vb_s, w_s, rb_s, rk_s, gs_s, bg_s, bkt_s, ut_s, pcol_s, y_s):
    tt = r_ref.shape[1]
    c = CHUNK
    nq = WKV_BLOCK_CHUNKS
    blk = nq * c
    pair = 2 * HEAD_DIM
    n_pairs = N_HEADS // 2
    quad = 2 * pair
    lanes = [slice(p * pair, (p + 1) * pair) for p in range(n_pairs)]
    step = pl.program_id(1)
    wslot = step % 2
    rslot = 1 - wslot
    slotted = (rt_s, vb_s, w_s, rb_s, rk_s, gs_s, bg_s, bkt_s, ut_s, pcol_s)

    @pl.when(step == 0)
    def _():
        s_ref[...] = jnp.zeros_like(s_ref)
        for ref in slotted:
            ref[1] = jnp.zeros(ref.shape[1:], ref.dtype)

    def iota(shape, dim):
        return lax.broadcasted_iota(jnp.int32, shape, dim)

    tri2 = ((iota((blk, blk), 0) >= iota((blk, blk), 1))
            & ((iota((blk, blk), 0) // c) == (iota((blk, blk), 1) // c))).astype(BF16)
    rowc = iota((c, pair), 0)
    colc = iota((c, pair), 1) % c
    incl2 = rowc >= colc
    strict2 = rowc > colc
    lane_lo = iota((c, pair), 1) < HEAD_DIM
    eye4 = (iota((c, quad), 0) == (iota((c, quad), 1) % c)).astype(F32)
    bd4 = (iota((quad, quad), 0) // c) == (iota((quad, quad), 1) // c)
    bd2 = (iota((pair, pair), 0) // HEAD_DIM) == (iota((pair, pair), 1) // HEAD_DIM)
    zero_b = jnp.zeros((c, pair), BF16)
    hones = hones_ref[...]

    def head_stack(x):
        return jnp.concatenate([jnp.where(lane_lo, x, zero_b), jnp.where(lane_lo, zero_b, x)], axis=0)

    def block_diag4(p4):
        pb = p4.astype(BF16)
        return jnp.where(bd4, jnp.concatenate([pb] * 4, axis=0), jnp.zeros((quad, quad), BF16))

    def head_sum(x):
        return _head_sum(x.astype(BF16), hones)

    def phase1(bi):
        base = bi * blk
        rows = slice(base, base + blk)
        r = r_ref[0, rows, :]
        k = k_ref[0, rows, :]
        v = v_ref[0, rows, :]
        kk = kk_ref[0, rows, :]
        lw = lw_ref[0, rows, :]
        g = g_ref[0, rows, :]
        vb = v.astype(BF16)
        beta = kk * a_ref[0, rows, :]
        lcum = sum(_dot(tri2, part) for part in _split_bf16(lw, 2))
        bonus = head_sum(r * k * rk_ref[...]) * v
        yield

        l_end = [lcum[(q + 1) * c - 1:(q + 1) * c, :] for q in range(nq)]
        chunk_of_row = iota(lcum.shape, 0) // c
        lend = l_end[0]
        for q in range(1, nq):
            lend = jnp.where(chunk_of_row == q, l_end[q], lend)
        e_neg = jnp.exp(-lcum)
        e_end = jnp.exp(lend - lcum)
        r_t = (r * jnp.exp(lcum)).astype(BF16)
        a_t = (-kk * jnp.exp(lcum - lw)).astype(BF16)
        k_t = (k * e_neg).astype(BF16)
        b_t = (beta * e_neg).astype(BF16)
        b_e = beta * e_end
        k_e = k * e_end
        rt_s[wslot, rows, :] = r_t
        vb_s[wslot, rows, :] = vb
        gs_s[wslot, rows, :] = g.astype(BF16)
        bg_s[wslot, rows, :] = (bonus * g).astype(BF16)

        units = [(q, p) for q in range(nq) for p in range(n_pairs)]
        p_end = [jnp.exp(le) for le in l_end]
        n_rows = -(-len(units) // SUBLANES) * SUBLANES
        sub = iota((n_rows, pair), 0)
        p_rows = jnp.zeros((n_rows, pair), F32)
        for j, (q, p) in enumerate(units):
            p_rows = jnp.where(sub == j, p_end[q][:, lanes[p]], p_rows)
        p_cols = jnp.concatenate([p_rows, jnp.zeros((pair - n_rows, pair), F32)], axis=0).T
        for j, (q, p) in enumerate(units):
            pcol_s[wslot, bi * nq + q, :, lanes[p]] = jnp.broadcast_to(p_cols[:, j:j + 1], (pair, pair))

        ab, ak = {}, {}
        for q, p in units:
            rs = slice(q * c, (q + 1) * c)
            ls = lanes[p]
            lhs = jnp.concatenate([a_t[rs, ls], r_t[rs, ls]], axis=0)
            rhs = jnp.concatenate([head_stack(b_t[rs, ls]), head_stack(k_t[rs, ls])], axis=0)
            out = _dot_nt(lhs, rhs)
            ab[q, p] = jnp.where(strict2, out[:c, :pair], 0.0)
            ak[q, p] = jnp.where(strict2, out[:c, pair:], 0.0).astype(BF16)
            qrows = slice(base + q * c, base + (q + 1) * c)
            rb_s[wslot, qrows, ls] = jnp.where(incl2, out[c:, :pair], 0.0).astype(BF16)
            rk_s[wslot, qrows, ls] = jnp.where(incl2, out[c:, pair:], 0.0).astype(BF16)
            bk = jnp.concatenate([b_e[rs, ls], k_e[rs, ls]], axis=0)
            bkt_s[wslot, 2 * base + 2 * q * c:2 * base + 2 * (q + 1) * c, ls] = bk.T.astype(BF16)
        yield

        quads = [(q, hf) for q in range(nq) for hf in range(n_pairs // 2)]
        pw = {u: jnp.concatenate([ab[u[0], 2 * u[1]], ab[u[0], 2 * u[1] + 1]], axis=1) for u in quads}
        tinv = {u: eye4 + pw[u] for u in quads}
        pw = {u: _dot(pw[u].astype(BF16), block_diag4(pw[u])) for u in quads}
        yield
        for _ in range(int(math.log2(c)) - 2):
            for u in quads:
                out = _dot(jnp.concatenate([pw[u].astype(BF16), tinv[u].astype(BF16)], axis=0),
                           block_diag4(pw[u]))
                pw[u] = out[:c]
                tinv[u] = tinv[u] + out[c:]
            yield
        tinv = {u: (tinv[u] + _dot(tinv[u].astype(BF16), block_diag4(pw[u]))).astype(BF16)
                for u in quads}
        yield

        aakv = {}
        for q, p in units:
            rs = slice(q * c, (q + 1) * c)
            aakv[q, p] = _dot(ak[q, p], head_stack(vb[rs, lanes[p]])).astype(BF16)
        yield
        for q, p in units:
            rs = slice(q * c, (q + 1) * c)
            ls = lanes[p]
            t2 = tinv[q, p // 2][:, (p % 2) * pair:(p % 2 + 1) * pair]
            rhs = jnp.concatenate([head_stack(a_t[rs, ls]), head_stack(aakv[q, p])], axis=1)
            wu = _dot(t2, rhs)
            qrows = slice(base + q * c, base + (q + 1) * c)
            w_s[wslot, qrows, ls] = wu[:, :pair].astype(BF16)
            ut_s[wslot, qrows, ls] = wu[:, pair:]

    def phase2(bi, z):
        pending = None
        for q in range(nq):
            ci = bi * nq + q
            rows = slice(ci * c, (ci + 1) * c)
            krows = slice(2 * ci * c, 2 * (ci + 1) * c)
            wr = [_dot(jnp.concatenate([w_s[rslot, rows, ls], rt_s[rslot, rows, ls]], axis=0),
                       z[p].astype(BF16)) for p, ls in enumerate(lanes)]
            if pending is not None:
                pending()
            yield
            ub = [(wr[p][:c] + ut_s[rslot, rows, ls]).astype(BF16) for p, ls in enumerate(lanes)]
            vb = [vb_s[rslot, rows, ls] for ls in lanes]
            upd = [_dot(bkt_s[rslot, krows, ls], jnp.concatenate([ub[p], vb[p]], axis=0))
                   for p, ls in enumerate(lanes)]
            for p, ls in enumerate(lanes):
                z[p] = z[p] * pcol_s[rslot, ci, :, ls] + jnp.where(bd2, upd[p], 0.0)

            def emit_y(rows=rows, wr=wr, ub=ub, vb=vb):
                for p, ls in enumerate(lanes):
                    y_s[rows, ls] = wr[p][c:] + _dot(
                        jnp.concatenate([rb_s[rslot, rows, ls], rk_s[rslot, rows, ls]], axis=1),
                        jnp.concatenate([head_stack(ub[p]), head_stack(vb[p])], axis=0))
            pending = emit_y
            yield
        pending()

    def phase3(rows):
        y = y_s[rows, :]
        mu = head_sum(y) * (1.0 / HEAD_DIM)
        yc = y - mu
        var = head_sum(yc * yc) * (1.0 / HEAD_DIM)
        yn = yc * lax.rsqrt(var + GN_EPS) * lnw_ref[...] + lnb_ref[...]
        yb_ref[0, rows, :] = (yn * gs_s[rslot, rows, :] + bg_s[rslot, rows, :]).astype(yb_ref.dtype)

    z = [s_ref[p] for p in range(n_pairs)]
    for bi in range(tt // blk):
        _interleave(phase1(bi), phase2(bi, z))
        for half in range(blk // (2 * c)):
            phase3(slice(bi * blk + half * 2 * c, bi * blk + (half + 1) * 2 * c))
    for p in range(n_pairs):
        s_ref[p] = z[p]


def _wkv(r, k, v, kkn, a, lw, g, lnw, lnb, rk):
    b, t, n = r.shape
    tt = TT_WKV
    nt = t // tt
    tok_in = pl.BlockSpec((1, tt, n), lambda i, j: (i, jnp.minimum(j, nt - 1), 0))
    tok_out = pl.BlockSpec((1, tt, n), lambda i, j: (i, jnp.maximum(j - 1, 0), 0))
    vec = pl.BlockSpec((1, n), lambda i, j: (0, 0))
    hones = _head_ones()
    bf_tile = pltpu.VMEM((2, tt, n), BF16)
    return pl.pallas_call(
        _wkv_kernel,
        grid=(b, nt + 1),
        in_specs=[tok_in] * 7 + [vec] * 3 + [pl.BlockSpec(hones.shape, lambda i, j: (0, 0))],
        out_specs=tok_out,
        out_shape=jax.ShapeDtypeStruct((b, t, n), BF16),
        scratch_shapes=[pltpu.VMEM((N_HEADS // 2, 2 * HEAD_DIM, 2 * HEAD_DIM), F32)]
        + [bf_tile] * 7 + [pltpu.VMEM((2, 2 * tt, n), BF16), pltpu.VMEM((2, tt, n), F32),
                           pltpu.VMEM((2, tt // CHUNK, 2 * HEAD_DIM, n), F32),
                           pltpu.VMEM((tt, n), F32)],
        compiler_params=pltpu.CompilerParams(
            dimension_semantics=("arbitrary", "arbitrary"), vmem_limit_bytes=VMEM_LIMIT),
        name="wkv",
    )(r, k, v, kkn, a, lw, g, lnw, lnb, rk, hones)


def _gelu_tanh(z):
    return 0.5 * z * (1.0 + jnp.tanh(math.sqrt(2.0 / math.pi) * (z + 0.044715 * (z * z * z))))


def _out_ffn_kernel(x_ref, ya_ref, yb_ref, wout_ref, gpost_ref, gpre2_ref, wup_ref, fcw_ref,
                    fcb_ref, wdown_ref, gpost2_ref, o_ref, cf_ref):
    tm = x_ref.shape[1]
    sub = TM_SUB

    @pl.when(pl.program_id(1) == 0)
    def _():
        cf_ref[...] = jnp.zeros_like(cf_ref)

    subs = [pl.ds(i * sub, sub) for i in range(tm // sub)]
    mixes = [_dot(jnp.concatenate([ya_ref[0, rs, :], yb_ref[0, rs, :]], axis=1), wout_ref[...])
             for rs in subs]

    def conv_cols(f, lo):
        cols = slice(lo, lo + FF_COLS)
        prev = cf_ref[:, cols]
        cw = fcw_ref[:, cols]
        out = (cw[0:1] * _shift_rows(f, prev, 2) + cw[1:2] * _shift_rows(f, prev, 1)
               + cw[2:3] * f + fcb_ref[:, cols])
        cf_ref[:, cols] = f[sub - SUBLANES:]
        return out

    n_groups = D_FF // FF_COLS
    for rs, mix in zip(subs, mixes):
        h = x_ref[0, rs, :] + _rms(mix, gpost_ref[...])
        hn = _rms(h, gpre2_ref[...]).astype(BF16)

        def up_proj(j, hn=hn):
            return [_dot(hn, wup_ref[:, lo:lo + FF_COLS]) for lo in (j * FF_COLS, D_FF + j * FF_COLS)]

        acc = None
        nxt = up_proj(0)
        for j in range(n_groups):
            f_gate, f_up = nxt
            if j + 1 < n_groups:
                nxt = up_proj(j + 1)
            gate = conv_cols(f_gate, j * FF_COLS)
            up = conv_cols(f_up, D_FF + j * FF_COLS)
            act = (_gelu_tanh(gate) * up).astype(BF16)
            part = _dot(act, wdown_ref[j * FF_COLS:(j + 1) * FF_COLS, :])
            acc = part if acc is None else acc + part
        o_ref[0, rs, :] = h + _rms(acc, gpost2_ref[...])


def _out_ffn(x, ya, yb, wout, gpost, gpre2, wup, fcw, fcb, wdown, gpost2):
    b, t, d = x.shape
    tm = TM_OUT
    tok = lambda n: pl.BlockSpec((1, tm, n), lambda i, j: (i, j, 0))
    full = lambda arr: pl.BlockSpec(arr.shape, lambda i, j: (0,) * arr.ndim)
    params = (wout, gpost, gpre2, wup, fcw, fcb, wdown, gpost2)
    return pl.pallas_call(
        _out_ffn_kernel,
        grid=(b, t // tm),
        in_specs=[tok(d), tok(CONV_DIM), tok(RWKV_DIM)] + [full(p) for p in params],
        out_specs=tok(d),
        out_shape=jax.ShapeDtypeStruct((b, t, d), F32),
        scratch_shapes=[pltpu.VMEM((SUBLANES, 2 * D_FF), F32)],
        compiler_params=pltpu.CompilerParams(
            dimension_semantics=("arbitrary", "arbitrary"), vmem_limit_bytes=VMEM_LIMIT),
        name="out_ffn",
    )(x, ya, yb, *params)


def _row(vec):
    return vec.reshape(1, -1).astype(F32)


def _head_ones():
    head = jnp.arange(HEAD_ONES) // HEAD_DIM
    return (head[:, None] == head[None, :]).astype(BF16)


def _layer(h, pre_mix_g, w_in, conv_a_w, shift_mu, w0, w2, a0, a2, g2, k_k, k_a, r_k, lnx_w, lnx_b,
           w_out, post_mix_g, pre_ffn_g, w_up, ffn_conv_w, ffn_conv_b, w_down, post_ffn_g):
    ca = 3 * CONV_DIM
    rb = ca + 3 * RWKV_DIM
    gate_pad = GATE_PAD - GATE_RANK
    wa = w_in[:, :ca].astype(BF16)
    wb = w_in[:, ca:rb].astype(BF16)
    wc = jnp.pad(w_in[:, rb:], ((0, 0), (0, gate_pad))).astype(BF16)
    mub = _row(shift_mu[:3 * RWKV_DIM])
    muc = _row(jnp.pad(shift_mu[3 * RWKV_DIM:], (0, gate_pad)))
    w2a2 = jnp.zeros((DECAY_RANK + ICLR_RANK, 2 * RWKV_DIM), F32)
    w2a2 = w2a2.at[:DECAY_RANK, :RWKV_DIM].set(w2).at[DECAY_RANK:, RWKV_DIM:].set(a2).astype(BF16)
    g2p = jnp.pad(g2, ((0, gate_pad), (0, 0))).astype(BF16)
    hsum = _head_ones()

    ya, r, k, v, kkn, a, lw, g = _mix_in(
        h, _row(pre_mix_g), wa, wb, wc, mub, muc, conv_a_w.astype(F32), _row(w0), _row(a0), w2a2,
        g2p, _row(k_k), _row(k_a), hsum)
    yb = _wkv(r, k, v, kkn, a, lw, g, _row(lnx_w), _row(lnx_b), _row(r_k))
    return _out_ffn(h, ya, yb, w_out.astype(BF16), _row(post_mix_g), _row(pre_ffn_g),
                    w_up.astype(BF16), ffn_conv_w.astype(F32), _row(ffn_conv_b),
                    w_down.astype(BF16), _row(post_ffn_g))


def kernel(x, pre_mix_g, w_in, conv_a_w, shift_mu, w0, w2, a0, a2, g2, k_k, k_a, r_k, lnx_w, lnx_b,
           w_out, post_mix_g, pre_ffn_g, w_up, ffn_conv_w, ffn_conv_b, w_down, post_ffn_g):
    h = x
    for l in range(pre_mix_g.shape[0]):
        h = _layer(h, pre_mix_g[l], w_in[l], conv_a_w[l], shift_mu[l], w0[l], w2[l], a0[l], a2[l],
                   g2[l], k_k[l], k_a[l], r_k[l], lnx_w[l], lnx_b[l], w_out[l], post_mix_g[l],
                   pre_ffn_g[l], w_up[l], ffn_conv_w[l], ffn_conv_b[l], w_down[l], post_ffn_g[l])
    return h
```

```python
import functools
import math

import jax
import jax.numpy as jnp
from jax import lax
from jax.experimental import pallas as pl
from jax.experimental.pallas import tpu as pltpu

D_MODEL = 1024
HEAD_DIM = 64
N_HEADS = 8
CONV_DIM = 512
RWKV_DIM = 512
DECAY_RANK = 64
ICLR_RANK = 64
GATE_RANK = 160
GATE_PAD = 256
D_FF = 2816
RMS_EPS = 1e-6
GN_EPS = HEAD_DIM * 1e-5

SUBLANES = 8
CHUNK = 64
WKV_BLOCK_CHUNKS = 4
TM_IN = 512
TT_WKV = 512
TM_OUT = 512
TM_SUB = 256
FF_COLS = 256
HEAD_ONES = 256
VMEM_LIMIT = 56 * 1024 * 1024

F32 = jnp.float32
BF16 = jnp.bfloat16
HIGHEST = lax.Precision.HIGHEST


def _dot(a, b):
    return jnp.dot(a, b, preferred_element_type=F32)


def _dot_nt(a, b):
    return lax.dot_general(a, b, (((1,), (1,)), ((), ())), preferred_element_type=F32)


def _dot_tn(a, b):
    return lax.dot_general(a, b, (((0,), (0,)), ((), ())), preferred_element_type=F32)


def _rms(z, g):
    return z * lax.rsqrt(jnp.mean(z * z, axis=-1, keepdims=True) + RMS_EPS) * g


def _sigmoid(z):
    return 1.0 / (1.0 + jnp.exp(-z))


def _head_sum(xb, hones):
    m = hones.shape[0]
    rows = xb.shape[0]
    slabs = xb.shape[1] // m
    out = _dot(jnp.concatenate([xb[:, j * m:(j + 1) * m] for j in range(slabs)], axis=0), hones)
    return jnp.concatenate([out[j * rows:(j + 1) * rows] for j in range(slabs)], axis=1)


def _shift_rows(z, prev, s):
    zs = pltpu.roll(z, s, 0)
    ps = pltpu.roll(prev, s, 0)
    row = lax.broadcasted_iota(jnp.int32, prev.shape, 0)
    head = jnp.where(row < s, ps, zs[:SUBLANES])
    return jnp.concatenate([head, zs[SUBLANES:]], axis=0)


def _mix_in_kernel(x_ref, gpre_ref, wa_ref, wb_ref, wc_ref, mub_ref, muc_ref, cw_ref, w0_ref,
                   a0_ref, w2a2_ref, g2_ref, kk_ref, ka_ref, hsum_ref,
                   ya_ref, r_ref, k_ref, v_ref, kkn_ref, a_ref, lw_ref, g_ref,
                   cu_ref, cqb_ref, cqc_ref):
    tm = x_ref.shape[1]
    sub = TM_SUB

    @pl.when(pl.program_id(1) == 0)
    def _():
        cu_ref[...] = jnp.zeros_like(cu_ref)
        cqb_ref[...] = jnp.zeros_like(cqb_ref)
        cqc_ref[...] = jnp.zeros_like(cqc_ref)

    subs = [pl.ds(i * sub, sub) for i in range(tm // sub)]
    proj = []
    for rs in subs:
        xn = _rms(x_ref[0, rs, :], gpre_ref[...]).astype(BF16)
        proj.append((_dot(xn, wa_ref[...]), _dot(xn, wb_ref[...]), _dot(xn, wc_ref[...])))

    for rs, (pa, qb, qc) in zip(subs, proj):
        u = pa[:, :CONV_DIM] * pa[:, 2 * CONV_DIM:]
        cu = cu_ref[...]
        cw = cw_ref[...]
        conv = (cw[0:1] * _shift_rows(u, cu, 2) + cw[1:2] * _shift_rows(u, cu, 1) + cw[2:3] * u)
        ya_ref[0, rs, :] = (pa[:, CONV_DIM:2 * CONV_DIM] * conv).astype(ya_ref.dtype)
        cu_ref[...] = u[sub - SUBLANES:]

        qbs = _shift_rows(qb, cqb_ref[...], 1)
        cqb_ref[...] = qb[sub - SUBLANES:]
        qb = qb + (qbs - qb) * mub_ref[...]
        qcs = _shift_rows(qc, cqc_ref[...], 1)
        cqc_ref[...] = qc[sub - SUBLANES:]
        qc = qc + (qcs - qc) * muc_ref[...]

        r = qb[:, :RWKV_DIM]
        k = qb[:, RWKV_DIM:2 * RWKV_DIM]
        v = qb[:, 2 * RWKV_DIM:]
        da = qc[:, :DECAY_RANK + ICLR_RANK]
        lane = lax.broadcasted_iota(jnp.int32, da.shape, 1)
        da = jnp.where(lane < DECAY_RANK, jnp.tanh(da), da)
        wa2 = _dot(da.astype(BF16), w2a2_ref[...])
        lw = -math.exp(-0.5) * _sigmoid(w0_ref[...] + wa2[:, :RWKV_DIM])
        a = _sigmoid(a0_ref[...] + wa2[:, RWKV_DIM:])
        g = _dot(_sigmoid(qc[:, DECAY_RANK + ICLR_RANK:]).astype(BF16), g2_ref[...])

        kk = k * kk_ref[...]
        ss = _head_sum((kk * kk).astype(BF16), hsum_ref[...])
        kkn = kk * lax.rsqrt(jnp.maximum(ss, 1e-24))
        k = k * (1.0 + (a - 1.0) * ka_ref[...])

        r_ref[0, rs, :] = r
        k_ref[0, rs, :] = k
        v_ref[0, rs, :] = v
        kkn_ref[0, rs, :] = kkn
        a_ref[0, rs, :] = a
        lw_ref[0, rs, :] = lw
        g_ref[0, rs, :] = g


def _mix_in(x, gpre, wa, wb, wc, mub, muc, cw, w0, a0, w2a2, g2p, k_k, k_a, hsum):
    b, t, d = x.shape
    tm = TM_IN
    grid = (b, t // tm)
    tok = lambda n: pl.BlockSpec((1, tm, n), lambda i, j: (i, j, 0))
    full = lambda arr: pl.BlockSpec(arr.shape, lambda i, j: (0,) * arr.ndim)
    params = (gpre, wa, wb, wc, mub, muc, cw, w0, a0, w2a2, g2p, k_k, k_a, hsum)
    out_sds = jax.ShapeDtypeStruct((b, t, RWKV_DIM), F32)
    return pl.pallas_call(
        _mix_in_kernel,
        grid=grid,
        in_specs=[tok(d)] + [full(p) for p in params],
        out_specs=[tok(RWKV_DIM)] * 8,
        out_shape=[jax.ShapeDtypeStruct((b, t, CONV_DIM), BF16)] + [out_sds] * 7,
        scratch_shapes=[pltpu.VMEM((SUBLANES, CONV_DIM), F32),
                        pltpu.VMEM((SUBLANES, 3 * RWKV_DIM), F32),
                        pltpu.VMEM((SUBLANES, wc.shape[1]), F32)],
        compiler_params=pltpu.CompilerParams(
            dimension_semantics=("arbitrary", "arbitrary"), vmem_limit_bytes=VMEM_LIMIT),
        name="mix_in",
    )(x, *params)


def _split_bf16(x, parts):
    out = []
    for _ in range(parts - 1):
        hi = x.astype(BF16)
        out.append(hi)
        x = x - hi.astype(F32)
    out.append(x.astype(BF16))
    return out


def _interleave(*stages):
    live = list(stages)
    while live:
        still = []
        for gen in live:
            try:
                next(gen)
                still.append(gen)
            except StopIteration:
                pass
        live = still


def _wkv_kernel(r_ref, k_ref, v_ref, kk_ref, a_ref, lw_ref, g_ref, lnw_ref, lnb_ref, rk_ref,
                hones_ref, yb_ref,
                s_ref, rt_s, vb_s, w_s, rb_s, rk_s, gs_s, bg_s, bkt_s, ut_s, pcol_s, y_s):
    tt = r_ref.shape[1]
    c = CHUNK
    nq = WKV_BLOCK_CHUNKS
    blk = nq * c
    pair = 2 * HEAD_DIM
    n_pairs = N_HEADS // 2
    quad = 2 * pair
    lanes = [slice(p * pair, (p + 1) * pair) for p in range(n_pairs)]
    step = pl.program_id(1)
    wslot = step % 2
    rslot = 1 - wslot
    slotted = (rt_s, vb_s, w_s, rb_s, rk_s, gs_s, bg_s, bkt_s, ut_s, pcol_s)

    @pl.when(step == 0)
    def _():
        s_ref[...] = jnp.zeros_like(s_ref)
        for ref in slotted:
            ref[1] = jnp.zeros(ref.shape[1:], ref.dtype)

    def iota(shape, dim):
        return lax.broadcasted_iota(jnp.int32, shape, dim)

    tri2 = ((iota((blk, blk), 0) >= iota((blk, blk), 1))
            & ((iota((blk, blk), 0) // c) == (iota((blk, blk), 1) // c))).astype(BF16)
    rowc = iota((c, pair), 0)
    colc = iota((c, pair), 1) % c
    incl2 = rowc >= colc
    strict2 = rowc > colc
    lane_lo = iota((c, pair), 1) < HEAD_DIM
    eye4 = (iota((c, quad), 0) == (iota((c, quad), 1) % c)).astype(F32)
    bd4 = (iota((quad, quad), 0) // c) == (iota((quad, quad), 1) // c)
    bd2 = (iota((pair, pair), 0) // HEAD_DIM) == (iota((pair, pair), 1) // HEAD_DIM)
    zero_b = jnp.zeros((c, pair), BF16)
    hones = hones_ref[...]

    def head_stack(x):
        return jnp.concatenate([jnp.where(lane_lo, x, zero_b), jnp.where(lane_lo, zero_b, x)], axis=0)

    def block_diag4(p4):
        pb = p4.astype(BF16)
        return jnp.where(bd4, jnp.concatenate([pb] * 4, axis=0), jnp.zeros((quad, quad), BF16))

    def head_sum(x):
        return _head_sum(x.astype(BF16), hones)

    def phase1(bi):
        base = bi * blk
        rows = slice(base, base + blk)
        r = r_ref[0, rows, :]
        k = k_ref[0, rows, :]
        v = v_ref[0, rows, :]
        kk = kk_ref[0, rows, :]
        lw = lw_ref[0, rows, :]
        g = g_ref[0, rows, :]
        vb = v.astype(BF16)
        beta = kk * a_ref[0, rows, :]
        lcum = sum(_dot(tri2, part) for part in _split_bf16(lw, 2))
        bonus = head_sum(r * k * rk_ref[...]) * v
        yield

        l_end = [lcum[(q + 1) * c - 1:(q + 1) * c, :] for q in range(nq)]
        chunk_of_row = iota(lcum.shape, 0) // c
        lend = l_end[0]
        for q in range(1, nq):
            lend = jnp.where(chunk_of_row == q, l_end[q], lend)
        e_neg = jnp.exp(-lcum)
        e_end = jnp.exp(lend - lcum)
        r_t = (r * jnp.exp(lcum)).astype(BF16)
        a_t = (-kk * jnp.exp(lcum - lw)).astype(BF16)
        k_t = (k * e_neg).astype(BF16)
        b_t = (beta * e_neg).astype(BF16)
        b_e = beta * e_end
        k_e = k * e_end
        rt_s[wslot, rows, :] = r_t
        vb_s[wslot, rows, :] = vb
        gs_s[wslot, rows, :] = g.astype(BF16)
        bg_s[wslot, rows, :] = (bonus * g).astype(BF16)

        units = [(q, p) for q in range(nq) for p in range(n_pairs)]
        p_end = [jnp.exp(le) for le in l_end]
        n_rows = -(-len(units) // SUBLANES) * SUBLANES
        sub = iota((n_rows, pair), 0)
        p_rows = jnp.zeros((n_rows, pair), F32)
        for j, (q, p) in enumerate(units):
            p_rows = jnp.where(sub == j, p_end[q][:, lanes[p]], p_rows)
        p_cols = jnp.concatenate([p_rows, jnp.zeros((pair - n_rows, pair), F32)], axis=0).T
        for j, (q, p) in enumerate(units):
            pcol_s[wslot, bi * nq + q, :, lanes[p]] = jnp.broadcast_to(p_cols[:, j:j + 1], (pair, pair))

        ab, ak = {}, {}
        for q, p in units:
            rs = slice(q * c, (q + 1) * c)
            ls = lanes[p]
            lhs = jnp.concatenate([a_t[rs, ls], r_t[rs, ls]], axis=0)
            rhs = jnp.concatenate([head_stack(b_t[rs, ls]), head_stack(k_t[rs, ls])], axis=0)
            out = _dot_nt(lhs, rhs)
            ab[q, p] = jnp.where(strict2, out[:c, :pair], 0.0)
            ak[q, p] = jnp.where(strict2, out[:c, pair:], 0.0).astype(BF16)
            qrows = slice(base + q * c, base + (q + 1) * c)
            rb_s[wslot, qrows, ls] = jnp.where(incl2, out[c:, :pair], 0.0).astype(BF16)
            rk_s[wslot, qrows, ls] = jnp.where(incl2, out[c:, pair:], 0.0).astype(BF16)
            bk = jnp.concatenate([b_e[rs, ls], k_e[rs, ls]], axis=0)
            bkt_s[wslot, 2 * base + 2 * q * c:2 * base + 2 * (q + 1) * c, ls] = bk.T.astype(BF16)
        yield

        quads = [(q, hf) for q in range(nq) for hf in range(n_pairs // 2)]
        pw = {u: jnp.concatenate([ab[u[0], 2 * u[1]], ab[u[0], 2 * u[1] + 1]], axis=1) for u in quads}
        tinv = {u: eye4 + pw[u] for u in quads}
        pw = {u: _dot(pw[u].astype(BF16), block_diag4(pw[u])) for u in quads}
        yield
        for _ in range(int(math.log2(c)) - 2):
            for u in quads:
                out = _dot(jnp.concatenate([pw[u].astype(BF16), tinv[u].astype(BF16)], axis=0),
                           block_diag4(pw[u]))
                pw[u] = out[:c]
                tinv[u] = tinv[u] + out[c:]
            yield
        tinv = {u: (tinv[u] + _dot(tinv[u].astype(BF16), block_diag4(pw[u]))).astype(BF16)
                for u in quads}
        yield

        aakv = {}
        for q, p in units:
            rs = slice(q * c, (q + 1) * c)
            aakv[q, p] = _dot(ak[q, p], head_stack(vb[rs, lanes[p]])).astype(BF16)
        yield
        for q, p in units:
            rs = slice(q * c, (q + 1) * c)
            ls = lanes[p]
            t2 = tinv[q, p // 2][:, (p % 2) * pair:(p % 2 + 1) * pair]
            rhs = jnp.concatenate([head_stack(a_t[rs, ls]), head_stack(aakv[q, p])], axis=1)
            wu = _dot(t2, rhs)
            qrows = slice(base + q * c, base + (q + 1) * c)
            w_s[wslot, qrows, ls] = wu[:, :pair].astype(BF16)
            ut_s[wslot, qrows, ls] = wu[:, pair:]

    def phase2(bi, z):
        pending = None
        for q in range(nq):
            ci = bi * nq + q
            rows = slice(ci * c, (ci + 1) * c)
            krows = slice(2 * ci * c, 2 * (ci + 1) * c)
            wr = [_dot(jnp.concatenate([w_s[rslot, rows, ls], rt_s[rslot, rows, ls]], axis=0),
                       z[p].astype(BF16)) for p, ls in enumerate(lanes)]
            if pending is not None:
                pending()
            yield
            ub = [(wr[p][:c] + ut_s[rslot, rows, ls]).astype(BF16) for p, ls in enumerate(lanes)]
            vb = [vb_s[rslot, rows, ls] for ls in lanes]
            upd = [_dot(bkt_s[rslot, krows, ls], jnp.concatenate([ub[p], vb[p]], axis=0))
                   for p, ls in enumerate(lanes)]
            for p, ls in enumerate(lanes):
                z[p] = z[p] * pcol_s[rslot, ci, :, ls] + jnp.where(bd2, upd[p], 0.0)

            def emit_y(rows=rows, wr=wr, ub=ub, vb=vb):
                for p, ls in enumerate(lanes):
                    y_s[rows, ls] = wr[p][c:] + _dot(
                        jnp.concatenate([rb_s[rslot, rows, ls], rk_s[rslot, rows, ls]], axis=1),
                        jnp.concatenate([head_stack(ub[p]), head_stack(vb[p])], axis=0))
            pending = emit_y
            yield
        pending()

    def phase3(rows):
        y = y_s[rows, :]
        mu = head_sum(y) * (1.0 / HEAD_DIM)
        yc = y - mu
        var = head_sum(yc * yc) * (1.0 / HEAD_DIM)
        yn = yc * lax.rsqrt(var + GN_EPS) * lnw_ref[...] + lnb_ref[...]
        yb_ref[0, rows, :] = (yn * gs_s[rslot, rows, :] + bg_s[rslot, rows, :]).astype(yb_ref.dtype)

    z = [s_ref[p] for p in range(n_pairs)]
    for bi in range(tt // blk):
        _interleave(phase1(bi), phase2(bi, z))
        for half in range(blk // (2 * c)):
            phase3(slice(bi * blk + half * 2 * c, bi * blk + (half + 1) * 2 * c))
    for p in range(n_pairs):
        s_ref[p] = z[p]


def _wkv(r, k, v, kkn, a, lw, g, lnw, lnb, rk):
    b, t, n = r.shape
    tt = TT_WKV
    nt = t // tt
    tok_in = pl.BlockSpec((1, tt, n), lambda i, j: (i, jnp.minimum(j, nt - 1), 0))
    tok_out = pl.BlockSpec((1, tt, n), lambda i, j: (i, jnp.maximum(j - 1, 0), 0))
    vec = pl.BlockSpec((1, n), lambda i, j: (0, 0))
    hones = _head_ones()
    bf_tile = pltpu.VMEM((2, tt, n), BF16)
    return pl.pallas_call(
        _wkv_kernel,
        grid=(b, nt + 1),
        in_specs=[tok_in] * 7 + [vec] * 3 + [pl.BlockSpec(hones.shape, lambda i, j: (0, 0))],
        out_specs=tok_out,
        out_shape=jax.ShapeDtypeStruct((b, t, n), BF16),
        scratch_shapes=[pltpu.VMEM((N_HEADS // 2, 2 * HEAD_DIM, 2 * HEAD_DIM), F32)]
        + [bf_tile] * 7 + [pltpu.VMEM((2, 2 * tt, n), BF16), pltpu.VMEM((2, tt, n), F32),
                           pltpu.VMEM((2, tt // CHUNK, 2 * HEAD_DIM, n), F32),
                           pltpu.VMEM((tt, n), F32)],
        compiler_params=pltpu.CompilerParams(
            dimension_semantics=("arbitrary", "arbitrary"), vmem_limit_bytes=VMEM_LIMIT),
        name="wkv",
    )(r, k, v, kkn, a, lw, g, lnw, lnb, rk, hones)


def _gelu_tanh_times(z, up):
    c0 = math.sqrt(2.0 / math.pi)
    zu = (0.5 * z) * up
    return zu + zu * jnp.tanh(z * (c0 + (c0 * 0.044715) * (z * z)))


def _out_ffn_kernel(x_ref, ya_ref, yb_ref, wout_ref, gpost_ref, gpre2_ref, wup_ref, fcw_ref,
                    fcb_ref, wdown_ref, gpost2_ref, o_ref, cf_ref):
    tm = x_ref.shape[1]
    sub = TM_SUB

    @pl.when(pl.program_id(1) == 0)
    def _():
        cf_ref[...] = jnp.zeros_like(cf_ref)

    subs = [pl.ds(i * sub, sub) for i in range(tm // sub)]
    mixes = [_dot(jnp.concatenate([ya_ref[0, rs, :], yb_ref[0, rs, :]], axis=1), wout_ref[...])
             for rs in subs]

    def conv_cols(f, lo):
        cols = slice(lo, lo + FF_COLS)
        prev = cf_ref[:, cols]
        cw = fcw_ref[:, cols]
        out = (cw[0:1] * _shift_rows(f, prev, 2) + cw[1:2] * _shift_rows(f, prev, 1)
               + cw[2:3] * f + fcb_ref[:, cols])
        cf_ref[:, cols] = f[sub - SUBLANES:]
        return out

    n_groups = D_FF // FF_COLS
    hs, ups = [], []
    for mix, rs in zip(mixes, subs):
        h = x_ref[0, rs, :] + _rms(mix, gpost_ref[...])
        hn = _rms(h, gpre2_ref[...]).astype(BF16)
        hs.append(h)
        ups.append([[_dot(hn, wup_ref[:, lo:lo + FF_COLS]) for lo in (j * FF_COLS, D_FF + j * FF_COLS)]
                    for j in range(n_groups)])
    for rs, h, up_s in zip(subs, hs, ups):
        acts = []
        for j, (f_gate, f_up) in enumerate(up_s):
            gate = conv_cols(f_gate, j * FF_COLS)
            up = conv_cols(f_up, D_FF + j * FF_COLS)
            acts.append(_gelu_tanh_times(gate, up).astype(BF16))
        ffn = _dot(jnp.concatenate(acts, axis=1), wdown_ref[...])
        o_ref[0, rs, :] = h + _rms(ffn, gpost2_ref[...])


def _out_ffn(x, ya, yb, wout, gpost, gpre2, wup, fcw, fcb, wdown, gpost2):
    b, t, d = x.shape
    tm = TM_OUT
    tok = lambda n: pl.BlockSpec((1, tm, n), lambda i, j: (i, j, 0))
    full = lambda arr: pl.BlockSpec(arr.shape, lambda i, j: (0,) * arr.ndim)
    params = (wout, gpost, gpre2, wup, fcw, fcb, wdown, gpost2)
    return pl.pallas_call(
        _out_ffn_kernel,
        grid=(b, t // tm),
        in_specs=[tok(d), tok(CONV_DIM), tok(RWKV_DIM)] + [full(p) for p in params],
        out_specs=tok(d),
        out_shape=jax.ShapeDtypeStruct((b, t, d), F32),
        scratch_shapes=[pltpu.VMEM((SUBLANES, 2 * D_FF), F32)],
        compiler_params=pltpu.CompilerParams(
            dimension_semantics=("arbitrary", "arbitrary"), vmem_limit_bytes=VMEM_LIMIT),
        name="out_ffn",
    )(x, ya, yb, *params)


def _row(vec):
    return vec.reshape(1, -1).astype(F32)


def _head_ones():
    head = jnp.arange(HEAD_ONES) // HEAD_DIM
    return (head[:, None] == head[None, :]).astype(BF16)


def _layer(h, pre_mix_g, w_in, conv_a_w, shift_mu, w0, w2, a0, a2, g2, k_k, k_a, r_k, lnx_w, lnx_b,
           w_out, post_mix_g, pre_ffn_g, w_up, ffn_conv_w, ffn_conv_b, w_down, post_ffn_g):
    ca = 3 * CONV_DIM
    rb = ca + 3 * RWKV_DIM
    gate_pad = GATE_PAD - GATE_RANK
    wa = w_in[:, :ca].astype(BF16)
    wb = w_in[:, ca:rb].astype(BF16)
    wc = jnp.pad(w_in[:, rb:], ((0, 0), (0, gate_pad))).astype(BF16)
    mub = _row(shift_mu[:3 * RWKV_DIM])
    muc = _row(jnp.pad(shift_mu[3 * RWKV_DIM:], (0, gate_pad)))
    w2a2 = jnp.zeros((DECAY_RANK + ICLR_RANK, 2 * RWKV_DIM), F32)
    w2a2 = w2a2.at[:DECAY_RANK, :RWKV_DIM].set(w2).at[DECAY_RANK:, RWKV_DIM:].set(a2).astype(BF16)
    g2p = jnp.pad(g2, ((0, gate_pad), (0, 0))).astype(BF16)
    hsum = _head_ones()

    ya, r, k, v, kkn, a, lw, g = _mix_in(
        h, _row(pre_mix_g), wa, wb, wc, mub, muc, conv_a_w.astype(F32), _row(w0), _row(a0), w2a2,
        g2p, _row(k_k), _row(k_a), hsum)
    yb = _wkv(r, k, v, kkn, a, lw, g, _row(lnx_w), _row(lnx_b), _row(r_k))
    return _out_ffn(h, ya, yb, w_out.astype(BF16), _row(post_mix_g), _row(pre_ffn_g),
                    w_up.astype(BF16), ffn_conv_w.astype(F32), _row(ffn_conv_b),
                    w_down.astype(BF16), _row(post_ffn_g))


def kernel(x, pre_mix_g, w_in, conv_a_w, shift_mu, w0, w2, a0, a2, g2, k_k, k_a, r_k, lnx_w, lnx_b,
           w_out, post_mix_g, pre_ffn_g, w_up, ffn_conv_w, ffn_conv_b, w_down, post_ffn_g):
    h = x
    for l in range(pre_mix_g.shape[0]):
        h = _layer(h, pre_mix_g[l], w_in[l], conv_a_w[l], shift_mu[l], w0[l], w2[l], a0[l], a2[l],
                   g2[l], k_k[l], k_a[l], r_k[l], lnx_w[l], lnx_b[l], w_out[l], post_mix_g[l],
                   pre_ffn_g[l], w_up[l], ffn_conv_w[l], ffn_conv_b[l], w_down[l], post_ffn_g[l])
    return h
```

```python
import math

import jax
import jax.numpy as jnp
from jax import lax
from jax.experimental import pallas as pl
from jax.experimental.pallas import tpu as pltpu

D_MODEL = 1024
HEAD_DIM = 64
N_HEADS = 8
CONV_DIM = 512
RWKV_DIM = 512
DECAY_RANK = 64
ICLR_RANK = 64
GATE_RANK = 160
GATE_PAD = 256
D_FF = 2816
RMS_EPS = 1e-6
GN_EPS = HEAD_DIM * 1e-5

SUBLANES = 8
CHUNK = 64
WKV_BLOCK_CHUNKS = 4
TM_IN = 512
TT_WKV = 512
TM_OUT = 512
TM_SUB = 256
FF_COLS = 256
HEAD_ONES = 256
VMEM_LIMIT = 56 * 1024 * 1024

F32 = jnp.float32
BF16 = jnp.bfloat16


def _dot(a, b):
    return jnp.dot(a, b, preferred_element_type=F32)


def _dot_nt(a, b):
    return lax.dot_general(a, b, (((1,), (1,)), ((), ())), preferred_element_type=F32)


def _rms(z, g):
    return z * lax.rsqrt(jnp.mean(z * z, axis=-1, keepdims=True) + RMS_EPS) * g


def _sigmoid(z):
    return 1.0 / (1.0 + jnp.exp(-z))


def _head_sum(xb, hones):
    m = hones.shape[0]
    rows = xb.shape[0]
    slabs = xb.shape[1] // m
    out = _dot(jnp.concatenate([xb[:, j * m:(j + 1) * m] for j in range(slabs)], axis=0), hones)
    return jnp.concatenate([out[j * rows:(j + 1) * rows] for j in range(slabs)], axis=1)


def _shift_rows(z, prev, s):
    zs = pltpu.roll(z, s, 0)
    ps = pltpu.roll(prev, s, 0)
    row = lax.broadcasted_iota(jnp.int32, prev.shape, 0)
    head = jnp.where(row < s, ps, zs[:SUBLANES])
    return jnp.concatenate([head, zs[SUBLANES:]], axis=0)


def _mix_in_kernel(x_ref, gpre_ref, wa_ref, wb_ref, wc_ref, mub_ref, muc_ref, cw_ref, w0_ref,
                   a0_ref, w2a2_ref, g2_ref, kk_ref, ka_ref, hsum_ref,
                   ya_ref, r_ref, k_ref, v_ref, kkn_ref, a_ref, lw_ref, g_ref,
                   cu_ref, cqb_ref, cqc_ref):
    tm = x_ref.shape[1]
    sub = TM_SUB

    @pl.when(pl.program_id(1) == 0)
    def _():
        cu_ref[...] = jnp.zeros_like(cu_ref)
        cqb_ref[...] = jnp.zeros_like(cqb_ref)
        cqc_ref[...] = jnp.zeros_like(cqc_ref)

    subs = [pl.ds(i * sub, sub) for i in range(tm // sub)]
    proj = []
    for rs in subs:
        xn = _rms(x_ref[0, rs, :], gpre_ref[...]).astype(BF16)
        proj.append((_dot(xn, wa_ref[...]), _dot(xn, wb_ref[...]), _dot(xn, wc_ref[...])))

    for rs, (pa, qb, qc) in zip(subs, proj):
        u = pa[:, :CONV_DIM] * pa[:, 2 * CONV_DIM:]
        cu = cu_ref[...]
        cw = cw_ref[...]
        conv = (cw[0:1] * _shift_rows(u, cu, 2) + cw[1:2] * _shift_rows(u, cu, 1) + cw[2:3] * u)
        ya_ref[0, rs, :] = (pa[:, CONV_DIM:2 * CONV_DIM] * conv).astype(ya_ref.dtype)
        cu_ref[...] = u[sub - SUBLANES:]

        qbs = _shift_rows(qb, cqb_ref[...], 1)
        cqb_ref[...] = qb[sub - SUBLANES:]
        qb = qb + (qbs - qb) * mub_ref[...]
        qcs = _shift_rows(qc, cqc_ref[...], 1)
        cqc_ref[...] = qc[sub - SUBLANES:]
        qc = qc + (qcs - qc) * muc_ref[...]

        r = qb[:, :RWKV_DIM]
        k = qb[:, RWKV_DIM:2 * RWKV_DIM]
        v = qb[:, 2 * RWKV_DIM:]
        da = qc[:, :DECAY_RANK + ICLR_RANK]
        lane = lax.broadcasted_iota(jnp.int32, da.shape, 1)
        da = jnp.where(lane < DECAY_RANK, jnp.tanh(da), da)
        wa2 = _dot(da.astype(BF16), w2a2_ref[...])
        lw = -math.exp(-0.5) * _sigmoid(w0_ref[...] + wa2[:, :RWKV_DIM])
        a = _sigmoid(a0_ref[...] + wa2[:, RWKV_DIM:])
        g = _dot(_sigmoid(qc[:, DECAY_RANK + ICLR_RANK:]).astype(BF16), g2_ref[...])

        kk = k * kk_ref[...]
        ss = _head_sum((kk * kk).astype(BF16), hsum_ref[...])
        kkn = kk * lax.rsqrt(jnp.maximum(ss, 1e-24))
        k = k * (1.0 + (a - 1.0) * ka_ref[...])

        r_ref[0, rs, :] = r
        k_ref[0, rs, :] = k
        v_ref[0, rs, :] = v
        kkn_ref[0, rs, :] = kkn
        a_ref[0, rs, :] = a
        lw_ref[0, rs, :] = lw
        g_ref[0, rs, :] = g


def _mix_in(x, gpre, wa, wb, wc, mub, muc, cw, w0, a0, w2a2, g2p, k_k, k_a, hsum):
    b, t, d = x.shape
    tm = TM_IN
    grid = (b, t // tm)
    tok = lambda n: pl.BlockSpec((1, tm, n), lambda i, j: (i, j, 0))
    full = lambda arr: pl.BlockSpec(arr.shape, lambda i, j: (0,) * arr.ndim)
    params = (gpre, wa, wb, wc, mub, muc, cw, w0, a0, w2a2, g2p, k_k, k_a, hsum)
    out_sds = jax.ShapeDtypeStruct((b, t, RWKV_DIM), F32)
    return pl.pallas_call(
        _mix_in_kernel,
        grid=grid,
        in_specs=[tok(d)] + [full(p) for p in params],
        out_specs=[tok(RWKV_DIM)] * 8,
        out_shape=[jax.ShapeDtypeStruct((b, t, CONV_DIM), BF16)] + [out_sds] * 7,
        scratch_shapes=[pltpu.VMEM((SUBLANES, CONV_DIM), F32),
                        pltpu.VMEM((SUBLANES, 3 * RWKV_DIM), F32),
                        pltpu.VMEM((SUBLANES, wc.shape[1]), F32)],
        compiler_params=pltpu.CompilerParams(
            dimension_semantics=("arbitrary", "arbitrary"), vmem_limit_bytes=VMEM_LIMIT),
        name="mix_in",
    )(x, *params)


def _split_bf16(x, parts):
    out = []
    for _ in range(parts - 1):
        hi = x.astype(BF16)
        out.append(hi)
        x = x - hi.astype(F32)
    out.append(x.astype(BF16))
    return out


def _interleave(*stages):
    live = list(stages)
    while live:
        still = []
        for gen in live:
            try:
                next(gen)
                still.append(gen)
            except StopIteration:
                pass
        live = still


def _wkv_kernel(r_ref, k_ref, v_ref, kk_ref, a_ref, lw_ref, g_ref, lnw_ref, lnb_ref, rk_ref,
                hones_ref, yb_ref,
                s_ref, rt_s, vb_s, w_s, rb_s, rk_s, gs_s, bg_s, bkt_s, ut_s, pcol_s, y_s):
    tt = r_ref.shape[1]
    c = CHUNK
    nq = WKV_BLOCK_CHUNKS
    blk = nq * c
    pair = 2 * HEAD_DIM
    n_pairs = N_HEADS // 2
    quad = 2 * pair
    lanes = [slice(p * pair, (p + 1) * pair) for p in range(n_pairs)]
    step = pl.program_id(1)
    wslot = step % 2
    rslot = 1 - wslot
    slotted = (rt_s, vb_s, w_s, rb_s, rk_s, gs_s, bg_s, bkt_s, ut_s, pcol_s)

    @pl.when(step == 0)
    def _():
        s_ref[...] = jnp.zeros_like(s_ref)
        for ref in slotted:
            ref[1] = jnp.zeros(ref.shape[1:], ref.dtype)

    def iota(shape, dim):
        return lax.broadcasted_iota(jnp.int32, shape, dim)

    tri2 = ((iota((blk, blk), 0) >= iota((blk, blk), 1))
            & ((iota((blk, blk), 0) // c) == (iota((blk, blk), 1) // c))).astype(BF16)
    rowc = iota((c, pair), 0)
    colc = iota((c, pair), 1) % c
    incl2 = rowc >= colc
    strict2 = rowc > colc
    lane_lo = iota((c, pair), 1) < HEAD_DIM
    eye4 = (iota((c, quad), 0) == (iota((c, quad), 1) % c)).astype(F32)
    bd4 = (iota((quad, quad), 0) // c) == (iota((quad, quad), 1) // c)
    bd2 = (iota((pair, pair), 0) // HEAD_DIM) == (iota((pair, pair), 1) // HEAD_DIM)
    zero_b = jnp.zeros((c, pair), BF16)
    hones = hones_ref[...]

    def head_stack(x):
        return jnp.concatenate([jnp.where(lane_lo, x, zero_b), jnp.where(lane_lo, zero_b, x)], axis=0)

    def block_diag4(p4):
        pb = p4.astype(BF16)
        return jnp.where(bd4, jnp.concatenate([pb] * 4, axis=0), jnp.zeros((quad, quad), BF16))

    def head_sum(x):
        return _head_sum(x.astype(BF16), hones)

    def phase1(bi):
        base = bi * blk
        rows = slice(base, base + blk)
        r = r_ref[0, rows, :]
        k = k_ref[0, rows, :]
        v = v_ref[0, rows, :]
        kk = kk_ref[0, rows, :]
        lw = lw_ref[0, rows, :]
        g = g_ref[0, rows, :]
        vb = v.astype(BF16)
        beta = kk * a_ref[0, rows, :]
        lcum = _dot(jnp.concatenate([tri2, tri2], axis=1), jnp.concatenate(_split_bf16(lw, 2), axis=0))
        bonus = head_sum(r * k * rk_ref[...]) * v
        yield

        l_end = [lcum[(q + 1) * c - 1:(q + 1) * c, :] for q in range(nq)]
        chunk_of_row = iota(lcum.shape, 0) // c
        lend = l_end[0]
        for q in range(1, nq):
            lend = jnp.where(chunk_of_row == q, l_end[q], lend)
        e_neg = jnp.exp(-lcum)
        e_end = jnp.exp(lend - lcum)
        r_t = (r * jnp.exp(lcum)).astype(BF16)
        a_t = (-kk * jnp.exp(lcum - lw)).astype(BF16)
        k_t = (k * e_neg).astype(BF16)
        b_t = (beta * e_neg).astype(BF16)
        b_e = beta * e_end
        k_e = k * e_end
        rt_s[wslot, rows, :] = r_t
        vb_s[wslot, rows, :] = vb
        gs_s[wslot, rows, :] = g.astype(BF16)
        bg_s[wslot, rows, :] = (bonus * g).astype(BF16)

        units = [(q, p) for q in range(nq) for p in range(n_pairs)]
        p_end = [jnp.exp(le) for le in l_end]
        n_rows = -(-len(units) // SUBLANES) * SUBLANES
        sub = iota((n_rows, pair), 0)
        p_rows = jnp.zeros((n_rows, pair), F32)
        for j, (q, p) in enumerate(units):
            p_rows = jnp.where(sub == j, p_end[q][:, lanes[p]], p_rows)
        p_cols = jnp.concatenate([p_rows, jnp.zeros((pair - n_rows, pair), F32)], axis=0).T
        for j, (q, p) in enumerate(units):
            pcol_s[wslot, bi * nq + q, :, lanes[p]] = jnp.broadcast_to(p_cols[:, j:j + 1], (pair, pair))

        ab, ak = {}, {}
        for q, p in units:
            rs = slice(q * c, (q + 1) * c)
            ls = lanes[p]
            lhs = jnp.concatenate([a_t[rs, ls], r_t[rs, ls]], axis=0)
            rhs = jnp.concatenate([head_stack(b_t[rs, ls]), head_stack(k_t[rs, ls])], axis=0)
            out = _dot_nt(lhs, rhs)
            ab[q, p] = jnp.where(strict2, out[:c, :pair], 0.0)
            ak[q, p] = jnp.where(strict2, out[:c, pair:], 0.0).astype(BF16)
            qrows = slice(base + q * c, base + (q + 1) * c)
            rb_s[wslot, qrows, ls] = jnp.where(incl2, out[c:, :pair], 0.0).astype(BF16)
            rk_s[wslot, qrows, ls] = jnp.where(incl2, out[c:, pair:], 0.0).astype(BF16)
            bk = jnp.concatenate([b_e[rs, ls], k_e[rs, ls]], axis=0)
            bkt_s[wslot, 2 * base + 2 * q * c:2 * base + 2 * (q + 1) * c, ls] = bk.T.astype(BF16)
        yield

        quads = [(q, hf) for q in range(nq) for hf in range(n_pairs // 2)]
        pw = {u: jnp.concatenate([ab[u[0], 2 * u[1]], ab[u[0], 2 * u[1] + 1]], axis=1) for u in quads}
        tinv = {u: eye4 + pw[u] for u in quads}
        pw = {u: _dot(pw[u].astype(BF16), block_diag4(pw[u])) for u in quads}
        yield
        for _ in range(int(math.log2(c)) - 2):
            for u in quads:
                out = _dot(jnp.concatenate([pw[u].astype(BF16), tinv[u].astype(BF16)], axis=0),
                           block_diag4(pw[u]))
                pw[u] = out[:c]
                tinv[u] = tinv[u] + out[c:]
            yield
        tinv = {u: (tinv[u] + _dot(tinv[u].astype(BF16), block_diag4(pw[u]))).astype(BF16)
                for u in quads}
        yield

        aakv = {}
        for q, p in units:
            rs = slice(q * c, (q + 1) * c)
            aakv[q, p] = _dot(ak[q, p], head_stack(vb[rs, lanes[p]])).astype(BF16)
        yield
        for q, p in units:
            rs = slice(q * c, (q + 1) * c)
            ls = lanes[p]
            t2 = tinv[q, p // 2][:, (p % 2) * pair:(p % 2 + 1) * pair]
            rhs = jnp.concatenate([head_stack(a_t[rs, ls]), head_stack(aakv[q, p])], axis=1)
            wu = _dot(t2, rhs)
            qrows = slice(base + q * c, base + (q + 1) * c)
            w_s[wslot, qrows, ls] = wu[:, :pair].astype(BF16)
            ut_s[wslot, qrows, ls] = wu[:, pair:]

    def phase2(bi, z):
        pending = None
        for q in range(nq):
            ci = bi * nq + q
            rows = slice(ci * c, (ci + 1) * c)
            krows = slice(2 * ci * c, 2 * (ci + 1) * c)
            wr = [_dot(jnp.concatenate([w_s[rslot, rows, ls], rt_s[rslot, rows, ls]], axis=0),
                       z[p].astype(BF16)) for p, ls in enumerate(lanes)]
            if pending is not None:
                pending()
            yield
            ub = [(wr[p][:c] + ut_s[rslot, rows, ls]).astype(BF16) for p, ls in enumerate(lanes)]
            vb = [vb_s[rslot, rows, ls] for ls in lanes]
            upd = [_dot(bkt_s[rslot, krows, ls], jnp.concatenate([ub[p], vb[p]], axis=0))
                   for p, ls in enumerate(lanes)]
            for p, ls in enumerate(lanes):
                z[p] = z[p] * pcol_s[rslot, ci, :, ls] + jnp.where(bd2, upd[p], 0.0)

            def emit_y(rows=rows, wr=wr, ub=ub, vb=vb):
                for p, ls in enumerate(lanes):
                    y_s[rows, ls] = wr[p][c:] + _dot(
                        jnp.concatenate([rb_s[rslot, rows, ls], rk_s[rslot, rows, ls]], axis=1),
                        jnp.concatenate([head_stack(ub[p]), head_stack(vb[p])], axis=0))
            pending = emit_y
            yield
        pending()

    def phase3(rows):
        y = y_s[rows, :]
        mu = head_sum(y) * (1.0 / HEAD_DIM)
        yc = y - mu
        var = head_sum(yc * yc) * (1.0 / HEAD_DIM)
        yn = yc * lax.rsqrt(var + GN_EPS) * lnw_ref[...] + lnb_ref[...]
        yb_ref[0, rows, :] = (yn * gs_s[rslot, rows, :] + bg_s[rslot, rows, :]).astype(yb_ref.dtype)

    z = [s_ref[p] for p in range(n_pairs)]
    for bi in range(tt // blk):
        _interleave(phase1(bi), phase2(bi, z))
        for half in range(blk // (2 * c)):
            phase3(slice(bi * blk + half * 2 * c, bi * blk + (half + 1) * 2 * c))
    for p in range(n_pairs):
        s_ref[p] = z[p]


def _wkv(r, k, v, kkn, a, lw, g, lnw, lnb, rk):
    b, t, n = r.shape
    tt = TT_WKV
    nt = t // tt
    tok_in = pl.BlockSpec((1, tt, n), lambda i, j: (i, jnp.minimum(j, nt - 1), 0))
    tok_out = pl.BlockSpec((1, tt, n), lambda i, j: (i, jnp.maximum(j - 1, 0), 0))
    vec = pl.BlockSpec((1, n), lambda i, j: (0, 0))
    hones = _head_ones()
    bf_tile = pltpu.VMEM((2, tt, n), BF16)
    return pl.pallas_call(
        _wkv_kernel,
        grid=(b, nt + 1),
        in_specs=[tok_in] * 7 + [vec] * 3 + [pl.BlockSpec(hones.shape, lambda i, j: (0, 0))],
        out_specs=tok_out,
        out_shape=jax.ShapeDtypeStruct((b, t, n), BF16),
        scratch_shapes=[pltpu.VMEM((N_HEADS // 2, 2 * HEAD_DIM, 2 * HEAD_DIM), F32)]
        + [bf_tile] * 7 + [pltpu.VMEM((2, 2 * tt, n), BF16), pltpu.VMEM((2, tt, n), F32),
                           pltpu.VMEM((2, tt // CHUNK, 2 * HEAD_DIM, n), F32),
                           pltpu.VMEM((tt, n), F32)],
        compiler_params=pltpu.CompilerParams(
            dimension_semantics=("arbitrary", "arbitrary"), vmem_limit_bytes=VMEM_LIMIT),
        name="wkv",
    )(r, k, v, kkn, a, lw, g, lnw, lnb, rk, hones)


def _gelu_tanh_times(z, up):
    c0 = math.sqrt(2.0 / math.pi)
    zu = (0.5 * z) * up
    return zu + zu * jnp.tanh(z * (c0 + (c0 * 0.044715) * (z * z)))


def _out_ffn_kernel(x_ref, ya_ref, yb_ref, wout_ref, gpost_ref, gpre2_ref, wup_ref, fcw_ref,
                    fcb_ref, wdown_ref, gpost2_ref, o_ref, cf_ref):
    tm = x_ref.shape[1]
    sub = TM_SUB

    @pl.when(pl.program_id(1) == 0)
    def _():
        cf_ref[...] = jnp.zeros_like(cf_ref)

    subs = [pl.ds(i * sub, sub) for i in range(tm // sub)]
    mixes = [_dot(jnp.concatenate([ya_ref[0, rs, :], yb_ref[0, rs, :]], axis=1), wout_ref[...])
             for rs in subs]

    def conv_cols(f, lo):
        cols = slice(lo, lo + FF_COLS)
        prev = cf_ref[:, cols]
        cw = fcw_ref[:, cols]
        out = (cw[0:1] * _shift_rows(f, prev, 2) + cw[1:2] * _shift_rows(f, prev, 1)
               + cw[2:3] * f + fcb_ref[:, cols])
        cf_ref[:, cols] = f[sub - SUBLANES:]
        return out

    n_groups = D_FF // FF_COLS
    hs, ups = [], []
    for mix, rs in zip(mixes, subs):
        h = x_ref[0, rs, :] + _rms(mix, gpost_ref[...])
        hn = _rms(h, gpre2_ref[...]).astype(BF16)
        hs.append(h)
        ups.append([[_dot(hn, wup_ref[:, lo:lo + FF_COLS]) for lo in (j * FF_COLS, D_FF + j * FF_COLS)]
                    for j in range(n_groups)])
    for rs, h, up_s in zip(subs, hs, ups):
        acts = []
        for j, (f_gate, f_up) in enumerate(up_s):
            gate = conv_cols(f_gate, j * FF_COLS)
            up = conv_cols(f_up, D_FF + j * FF_COLS)
            acts.append(_gelu_tanh_times(gate, up).astype(BF16))
        ffn = _dot(jnp.concatenate(acts, axis=1), wdown_ref[...])
        o_ref[0, rs, :] = h + _rms(ffn, gpost2_ref[...])


def _out_ffn(x, ya, yb, wout, gpost, gpre2, wup, fcw, fcb, wdown, gpost2):
    b, t, d = x.shape
    tm = TM_OUT
    tok = lambda n: pl.BlockSpec((1, tm, n), lambda i, j: (i, j, 0))
    full = lambda arr: pl.BlockSpec(arr.shape, lambda i, j: (0,) * arr.ndim)
    params = (wout, gpost, gpre2, wup, fcw, fcb, wdown, gpost2)
    return pl.pallas_call(
        _out_ffn_kernel,
        grid=(b, t // tm),
        in_specs=[tok(d), tok(CONV_DIM), tok(RWKV_DIM)] + [full(p) for p in params],
        out_specs=tok(d),
        out_shape=jax.ShapeDtypeStruct((b, t, d), F32),
        scratch_shapes=[pltpu.VMEM((SUBLANES, 2 * D_FF), F32)],
        compiler_params=pltpu.CompilerParams(
            dimension_semantics=("arbitrary", "arbitrary"), vmem_limit_bytes=VMEM_LIMIT),
        name="out_ffn",
    )(x, ya, yb, *params)


def _row(vec):
    return vec.reshape(1, -1).astype(F32)


def _head_ones():
    head = jnp.arange(HEAD_ONES) // HEAD_DIM
    return (head[:, None] == head[None, :]).astype(BF16)


def _layer(h, pre_mix_g, w_in, conv_a_w, shift_mu, w0, w2, a0, a2, g2, k_k, k_a, r_k, lnx_w, lnx_b,
           w_out, post_mix_g, pre_ffn_g, w_up, ffn_conv_w, ffn_conv_b, w_down, post_ffn_g):
    ca = 3 * CONV_DIM
    rb = ca + 3 * RWKV_DIM
    gate_pad = GATE_PAD - GATE_RANK
    wa = w_in[:, :ca].astype(BF16)
    wb = w_in[:, ca:rb].astype(BF16)
    wc = jnp.pad(w_in[:, rb:], ((0, 0), (0, gate_pad))).astype(BF16)
    mub = _row(shift_mu[:3 * RWKV_DIM])
    muc = _row(jnp.pad(shift_mu[3 * RWKV_DIM:], (0, gate_pad)))
    w2a2 = jnp.zeros((DECAY_RANK + ICLR_RANK, 2 * RWKV_DIM), F32)
    w2a2 = w2a2.at[:DECAY_RANK, :RWKV_DIM].set(w2).at[DECAY_RANK:, RWKV_DIM:].set(a2).astype(BF16)
    g2p = jnp.pad(g2, ((0, gate_pad), (0, 0))).astype(BF16)
    hsum = _head_ones()

    ya, r, k, v, kkn, a, lw, g = _mix_in(
        h, _row(pre_mix_g), wa, wb, wc, mub, muc, conv_a_w.astype(F32), _row(w0), _row(a0), w2a2,
        g2p, _row(k_k), _row(k_a), hsum)
    yb = _wkv(r, k, v, kkn, a, lw, g, _row(lnx_w), _row(lnx_b), _row(r_k))
    return _out_ffn(h, ya, yb, w_out.astype(BF16), _row(post_mix_g), _row(pre_ffn_g),
                    w_up.astype(BF16), ffn_conv_w.astype(F32), _row(ffn_conv_b),
                    w_down.astype(BF16), _row(post_ffn_g))


def kernel(x, pre_mix_g, w_in, conv_a_w, shift_mu, w0, w2, a0, a2, g2, k_k, k_a, r_k, lnx_w, lnx_b,
           w_out, post_mix_g, pre_ffn_g, w_up, ffn_conv_w, ffn_conv_b, w_down, post_ffn_g):
    h = x
    for l in range(pre_mix_g.shape[0]):
        h = _layer(h, pre_mix_g[l], w_in[l], conv_a_w[l], shift_mu[l], w0[l], w2[l], a0[l], a2[l],
                   g2[l], k_k[l], k_a[l], r_k[l], lnx_w[l], lnx_b[l], w_out[l], post_mix_g[l],
                   pre_ffn_g[l], w_up[l], ffn_conv_w[l], ffn_conv_b[l], w_down[l], post_ffn_g[l])
    return h
```

```python
import functools
import math

import jax
import jax.numpy as jnp
from jax import lax
from jax.experimental import pallas as pl
from jax.experimental.pallas import tpu as pltpu

D_MODEL = 1024
HEAD_DIM = 64
N_HEADS = 8
CONV_DIM = 512
RWKV_DIM = 512
DECAY_RANK = 64
ICLR_RANK = 64
GATE_RANK = 160
GATE_PAD = 256
D_FF = 2816
RMS_EPS = 1e-6
GN_EPS = HEAD_DIM * 1e-5

SUBLANES = 8
CHUNK = 64
WKV_BLOCK_CHUNKS = 4
TM_IN = 512
TT_WKV = 512
TM_OUT = 512
TM_SUB = 256
FF_COLS = 256
HEAD_ONES = 256
VMEM_LIMIT = 56 * 1024 * 1024

F32 = jnp.float32
BF16 = jnp.bfloat16


def _dot(a, b):
    return jnp.dot(a, b, preferred_element_type=F32)


def _dot_nt(a, b):
    return lax.dot_general(a, b, (((1,), (1,)), ((), ())), preferred_element_type=F32)


def _rms(z, g):
    return z * lax.rsqrt(jnp.mean(z * z, axis=-1, keepdims=True) + RMS_EPS) * g


def _sigmoid(z):
    return 1.0 / (1.0 + jnp.exp(-z))


def _head_sum(xb, hones):
    m = hones.shape[0]
    rows = xb.shape[0]
    slabs = xb.shape[1] // m
    out = _dot(jnp.concatenate([xb[:, j * m:(j + 1) * m] for j in range(slabs)], axis=0), hones)
    return jnp.concatenate([out[j * rows:(j + 1) * rows] for j in range(slabs)], axis=1)


def _shift_rows(z, prev, s):
    zs = pltpu.roll(z, s, 0)
    ps = pltpu.roll(prev, s, 0)
    row = lax.broadcasted_iota(jnp.int32, prev.shape, 0)
    head = jnp.where(row < s, ps, zs[:SUBLANES])
    return jnp.concatenate([head, zs[SUBLANES:]], axis=0)


def _mix_in_kernel(x_ref, gpre_ref, wa_ref, wb_ref, wc_ref, mub_ref, muc_ref, cw_ref, w0_ref,
                   a0_ref, w2a2_ref, g2_ref, kk_ref, ka_ref, hsum_ref,
                   ya_ref, r_ref, k_ref, v_ref, kkn_ref, a_ref, lw_ref, g_ref,
                   cu_ref, cqb_ref, cqc_ref):
    tm = x_ref.shape[1]
    sub = TM_SUB

    @pl.when(pl.program_id(1) == 0)
    def _():
        cu_ref[...] = jnp.zeros_like(cu_ref)
        cqb_ref[...] = jnp.zeros_like(cqb_ref)
        cqc_ref[...] = jnp.zeros_like(cqc_ref)

    subs = [pl.ds(i * sub, sub) for i in range(tm // sub)]
    proj = []
    for rs in subs:
        xn = _rms(x_ref[0, rs, :], gpre_ref[...]).astype(BF16)
        proj.append((_dot(xn, wa_ref[...]), _dot(xn, wb_ref[...]), _dot(xn, wc_ref[...])))

    for rs, (pa, qb, qc) in zip(subs, proj):
        u = pa[:, :CONV_DIM] * pa[:, 2 * CONV_DIM:]
        cu = cu_ref[...]
        cw = cw_ref[...]
        conv = (cw[0:1] * _shift_rows(u, cu, 2) + cw[1:2] * _shift_rows(u, cu, 1) + cw[2:3] * u)
        ya_ref[0, rs, :] = (pa[:, CONV_DIM:2 * CONV_DIM] * conv).astype(ya_ref.dtype)
        cu_ref[...] = u[sub - SUBLANES:]

        qbs = _shift_rows(qb, cqb_ref[...], 1)
        cqb_ref[...] = qb[sub - SUBLANES:]
        qb = qb + (qbs - qb) * mub_ref[...]
        qcs = _shift_rows(qc, cqc_ref[...], 1)
        cqc_ref[...] = qc[sub - SUBLANES:]
        qc = qc + (qcs - qc) * muc_ref[...]

        r = qb[:, :RWKV_DIM]
        k = qb[:, RWKV_DIM:2 * RWKV_DIM]
        v = qb[:, 2 * RWKV_DIM:]
        da = qc[:, :DECAY_RANK + ICLR_RANK]
        lane = lax.broadcasted_iota(jnp.int32, da.shape, 1)
        da = jnp.where(lane < DECAY_RANK, jnp.tanh(da), da)
        wa2 = _dot(da.astype(BF16), w2a2_ref[...])
        lw = -math.exp(-0.5) * _sigmoid(w0_ref[...] + wa2[:, :RWKV_DIM])
        a = _sigmoid(a0_ref[...] + wa2[:, RWKV_DIM:])
        g = _dot(_sigmoid(qc[:, DECAY_RANK + ICLR_RANK:]).astype(BF16), g2_ref[...])

        kk = k * kk_ref[...]
        ss = _head_sum((kk * kk).astype(BF16), hsum_ref[...])
        kkn = kk * lax.rsqrt(jnp.maximum(ss, 1e-24))
        k = k * (1.0 + (a - 1.0) * ka_ref[...])

        r_ref[0, rs, :] = r
        k_ref[0, rs, :] = k
        v_ref[0, rs, :] = v
        kkn_ref[0, rs, :] = kkn
        a_ref[0, rs, :] = a
        lw_ref[0, rs, :] = lw
        g_ref[0, rs, :] = g


def _mix_in(x, gpre, wa, wb, wc, mub, muc, cw, w0, a0, w2a2, g2p, k_k, k_a, hsum):
    b, t, d = x.shape
    tm = TM_IN
    grid = (b, t // tm)
    tok = lambda n: pl.BlockSpec((1, tm, n), lambda i, j: (i, j, 0))
    full = lambda arr: pl.BlockSpec(arr.shape, lambda i, j: (0,) * arr.ndim)
    params = (gpre, wa, wb, wc, mub, muc, cw, w0, a0, w2a2, g2p, k_k, k_a, hsum)
    out_sds = jax.ShapeDtypeStruct((b, t, RWKV_DIM), F32)
    return pl.pallas_call(
        _mix_in_kernel,
        grid=grid,
        in_specs=[tok(d)] + [full(p) for p in params],
        out_specs=[tok(RWKV_DIM)] * 8,
        out_shape=[jax.ShapeDtypeStruct((b, t, CONV_DIM), BF16)] + [out_sds] * 7,
        scratch_shapes=[pltpu.VMEM((SUBLANES, CONV_DIM), F32),
                        pltpu.VMEM((SUBLANES, 3 * RWKV_DIM), F32),
                        pltpu.VMEM((SUBLANES, wc.shape[1]), F32)],
        compiler_params=pltpu.CompilerParams(
            dimension_semantics=("arbitrary", "arbitrary"), vmem_limit_bytes=VMEM_LIMIT),
        name="mix_in",
    )(x, *params)


def _split_bf16(x, parts):
    out = []
    for _ in range(parts - 1):
        hi = x.astype(BF16)
        out.append(hi)
        x = x - hi.astype(F32)
    out.append(x.astype(BF16))
    return out


def _interleave(*stages):
    live = list(stages)
    while live:
        still = []
        for gen in live:
            try:
                next(gen)
                still.append(gen)
            except StopIteration:
                pass
        live = still


def _wkv_kernel(tiles_per_seq, r_ref, k_ref, v_ref, kk_ref, a_ref, lw_ref, g_ref, lnw_ref, lnb_ref,
                rk_ref, hones_ref, yb_ref,
                s_ref, rt_s, vb_s, w_s, rb_s, rk_s, gs_s, bg_s, bkt_s, ut_s, pcol_s, y_s):
    tt = r_ref.shape[1]
    c = CHUNK
    nq = WKV_BLOCK_CHUNKS
    blk = nq * c
    pair = 2 * HEAD_DIM
    n_pairs = N_HEADS // 2
    quad = 2 * pair
    lanes = [slice(p * pair, (p + 1) * pair) for p in range(n_pairs)]
    step = pl.program_id(0)
    wslot = step % 2
    rslot = 1 - wslot
    slotted = (rt_s, vb_s, w_s, rb_s, rk_s, gs_s, bg_s, bkt_s, ut_s, pcol_s)

    @pl.when(step == 0)
    def _():
        s_ref[...] = jnp.zeros_like(s_ref)
        for ref in slotted:
            ref[1] = jnp.zeros(ref.shape[1:], ref.dtype)

    def iota(shape, dim):
        return lax.broadcasted_iota(jnp.int32, shape, dim)

    tri2 = ((iota((blk, blk), 0) >= iota((blk, blk), 1))
            & ((iota((blk, blk), 0) // c) == (iota((blk, blk), 1) // c))).astype(BF16)
    rowc = iota((c, pair), 0)
    colc = iota((c, pair), 1) % c
    incl2 = rowc >= colc
    strict2 = rowc > colc
    lane_lo = iota((c, pair), 1) < HEAD_DIM
    eye4 = (iota((c, quad), 0) == (iota((c, quad), 1) % c)).astype(F32)
    bd4 = (iota((quad, quad), 0) // c) == (iota((quad, quad), 1) // c)
    bd2 = (iota((pair, pair), 0) // HEAD_DIM) == (iota((pair, pair), 1) // HEAD_DIM)
    zero_b = jnp.zeros((c, pair), BF16)
    hones = hones_ref[...]

    def head_stack(x):
        return jnp.concatenate([jnp.where(lane_lo, x, zero_b), jnp.where(lane_lo, zero_b, x)], axis=0)

    def block_diag4(p4):
        pb = p4.astype(BF16)
        return jnp.where(bd4, jnp.concatenate([pb] * 4, axis=0), jnp.zeros((quad, quad), BF16))

    def head_sum(x):
        return _head_sum(x.astype(BF16), hones)

    def phase1(bi):
        base = bi * blk
        rows = slice(base, base + blk)
        r = r_ref[0, rows, :]
        k = k_ref[0, rows, :]
        v = v_ref[0, rows, :]
        kk = kk_ref[0, rows, :]
        lw = lw_ref[0, rows, :]
        g = g_ref[0, rows, :]
        vb = v.astype(BF16)
        beta = kk * a_ref[0, rows, :]
        lcum = _dot(jnp.concatenate([tri2, tri2], axis=1), jnp.concatenate(_split_bf16(lw, 2), axis=0))
        bonus = head_sum(r * k * rk_ref[...]) * v
        yield

        l_end = [lcum[(q + 1) * c - 1:(q + 1) * c, :] for q in range(nq)]
        chunk_of_row = iota(lcum.shape, 0) // c
        lend = l_end[0]
        for q in range(1, nq):
            lend = jnp.where(chunk_of_row == q, l_end[q], lend)
        e_neg = jnp.exp(-lcum)
        e_end = jnp.exp(lend - lcum)
        r_t = (r * jnp.exp(lcum)).astype(BF16)
        a_t = (-kk * jnp.exp(lcum - lw)).astype(BF16)
        k_t = (k * e_neg).astype(BF16)
        b_t = (beta * e_neg).astype(BF16)
        b_e = beta * e_end
        k_e = k * e_end
        rt_s[wslot, rows, :] = r_t
        vb_s[wslot, rows, :] = vb
        gs_s[wslot, rows, :] = g.astype(BF16)
        bg_s[wslot, rows, :] = (bonus * g).astype(BF16)

        units = [(q, p) for q in range(nq) for p in range(n_pairs)]
        p_end = [jnp.exp(le) for le in l_end]
        n_rows = -(-len(units) // SUBLANES) * SUBLANES
        sub = iota((n_rows, pair), 0)
        p_rows = jnp.zeros((n_rows, pair), F32)
        for j, (q, p) in enumerate(units):
            p_rows = jnp.where(sub == j, p_end[q][:, lanes[p]], p_rows)
        p_cols = jnp.concatenate([p_rows, jnp.zeros((pair - n_rows, pair), F32)], axis=0).T
        for j, (q, p) in enumerate(units):
            pcol_s[wslot, bi * nq + q, :, lanes[p]] = jnp.broadcast_to(p_cols[:, j:j + 1], (pair, pair))

        ab, ak = {}, {}
        for q, p in units:
            rs = slice(q * c, (q + 1) * c)
            ls = lanes[p]
            lhs = jnp.concatenate([a_t[rs, ls], r_t[rs, ls]], axis=0)
            rhs = jnp.concatenate([head_stack(b_t[rs, ls]), head_stack(k_t[rs, ls])], axis=0)
            out = _dot_nt(lhs, rhs)
            ab[q, p] = jnp.where(strict2, out[:c, :pair], 0.0)
            ak[q, p] = jnp.where(strict2, out[:c, pair:], 0.0).astype(BF16)
            qrows = slice(base + q * c, base + (q + 1) * c)
            rb_s[wslot, qrows, ls] = jnp.where(incl2, out[c:, :pair], 0.0).astype(BF16)
            rk_s[wslot, qrows, ls] = jnp.where(incl2, out[c:, pair:], 0.0).astype(BF16)
            bk = jnp.concatenate([b_e[rs, ls], k_e[rs, ls]], axis=0)
            bkt_s[wslot, 2 * base + 2 * q * c:2 * base + 2 * (q + 1) * c, ls] = bk.T.astype(BF16)
        yield

        quads = [(q, hf) for q in range(nq) for hf in range(n_pairs // 2)]
        pw = {u: jnp.concatenate([ab[u[0], 2 * u[1]], ab[u[0], 2 * u[1] + 1]], axis=1) for u in quads}
        tinv = {u: eye4 + pw[u] for u in quads}
        pw = {u: _dot(pw[u].astype(BF16), block_diag4(pw[u])) for u in quads}
        yield
        for _ in range(int(math.log2(c)) - 2):
            for u in quads:
                out = _dot(jnp.concatenate([pw[u].astype(BF16), tinv[u].astype(BF16)], axis=0),
                           block_diag4(pw[u]))
                pw[u] = out[:c]
                tinv[u] = tinv[u] + out[c:]
            yield
        tinv = {u: (tinv[u] + _dot(tinv[u].astype(BF16), block_diag4(pw[u]))).astype(BF16)
                for u in quads}
        yield

        aakv = {}
        for q, p in units:
            rs = slice(q * c, (q + 1) * c)
            aakv[q, p] = _dot(ak[q, p], head_stack(vb[rs, lanes[p]])).astype(BF16)
        yield
        for q, p in units:
            rs = slice(q * c, (q + 1) * c)
            ls = lanes[p]
            t2 = tinv[q, p // 2][:, (p % 2) * pair:(p % 2 + 1) * pair]
            rhs = jnp.concatenate([head_stack(a_t[rs, ls]), head_stack(aakv[q, p])], axis=1)
            wu = _dot(t2, rhs)
            qrows = slice(base + q * c, base + (q + 1) * c)
            w_s[wslot, qrows, ls] = wu[:, :pair].astype(BF16)
            ut_s[wslot, qrows, ls] = wu[:, pair:]

    def phase2(bi, z):
        pending = None
        for q in range(nq):
            ci = bi * nq + q
            rows = slice(ci * c, (ci + 1) * c)
            krows = slice(2 * ci * c, 2 * (ci + 1) * c)
            wr = [_dot(jnp.concatenate([w_s[rslot, rows, ls], rt_s[rslot, rows, ls]], axis=0),
                       z[p].astype(BF16)) for p, ls in enumerate(lanes)]
            if pending is not None:
                pending()
            yield
            ub = [(wr[p][:c] + ut_s[rslot, rows, ls]).astype(BF16) for p, ls in enumerate(lanes)]
            vb = [vb_s[rslot, rows, ls] for ls in lanes]
            upd = [_dot(bkt_s[rslot, krows, ls], jnp.concatenate([ub[p], vb[p]], axis=0))
                   for p, ls in enumerate(lanes)]
            for p, ls in enumerate(lanes):
                z[p] = z[p] * pcol_s[rslot, ci, :, ls] + jnp.where(bd2, upd[p], 0.0)

            def emit_y(rows=rows, wr=wr, ub=ub, vb=vb):
                for p, ls in enumerate(lanes):
                    y_s[rows, ls] = wr[p][c:] + _dot(
                        jnp.concatenate([rb_s[rslot, rows, ls], rk_s[rslot, rows, ls]], axis=1),
                        jnp.concatenate([head_stack(ub[p]), head_stack(vb[p])], axis=0))
            pending = emit_y
            yield
        pending()

    def phase3(rows):
        y = y_s[rows, :]
        mu = head_sum(y) * (1.0 / HEAD_DIM)
        yc = y - mu
        var = head_sum(yc * yc) * (1.0 / HEAD_DIM)
        yn = yc * lax.rsqrt(var + GN_EPS) * lnw_ref[...] + lnb_ref[...]
        yb_ref[0, rows, :] = (yn * gs_s[rslot, rows, :] + bg_s[rslot, rows, :]).astype(yb_ref.dtype)

    fresh = (step + tiles_per_seq - 1) % tiles_per_seq == 0
    z = [jnp.where(fresh, 0.0, s_ref[p]) for p in range(n_pairs)]
    for bi in range(tt // blk):
        _interleave(phase1(bi), phase2(bi, z))
        for half in range(blk // (2 * c)):
            phase3(slice(bi * blk + half * 2 * c, bi * blk + (half + 1) * 2 * c))
    for p in range(n_pairs):
        s_ref[p] = z[p]


def _wkv(r, k, v, kkn, a, lw, g, lnw, lnb, rk):
    b, t, n = r.shape
    tt = TT_WKV
    nt = t // tt
    last = b * nt - 1

    def tile(s):
        return (s // nt, s % nt, 0)

    tok_in = pl.BlockSpec((1, tt, n), lambda s: tile(jnp.minimum(s, last)))
    tok_out = pl.BlockSpec((1, tt, n), lambda s: tile(jnp.maximum(s - 1, 0)))
    vec = pl.BlockSpec((1, n), lambda s: (0, 0))
    hones = _head_ones()
    bf_tile = pltpu.VMEM((2, tt, n), BF16)
    return pl.pallas_call(
        functools.partial(_wkv_kernel, nt),
        grid=(b * nt + 1,),
        in_specs=[tok_in] * 7 + [vec] * 3 + [pl.BlockSpec(hones.shape, lambda s: (0, 0))],
        out_specs=tok_out,
        out_shape=jax.ShapeDtypeStruct((b, t, n), BF16),
        scratch_shapes=[pltpu.VMEM((N_HEADS // 2, 2 * HEAD_DIM, 2 * HEAD_DIM), F32)]
        + [bf_tile] * 7 + [pltpu.VMEM((2, 2 * tt, n), BF16), pltpu.VMEM((2, tt, n), F32),
                           pltpu.VMEM((2, tt // CHUNK, 2 * HEAD_DIM, n), F32),
                           pltpu.VMEM((tt, n), F32)],
        compiler_params=pltpu.CompilerParams(
            dimension_semantics=("arbitrary",), vmem_limit_bytes=VMEM_LIMIT),
        name="wkv",
    )(r, k, v, kkn, a, lw, g, lnw, lnb, rk, hones)


def _gelu_tanh_times(z, up):
    c0 = math.sqrt(2.0 / math.pi)
    zu = (0.5 * z) * up
    return zu + zu * jnp.tanh(z * (c0 + (c0 * 0.044715) * (z * z)))


def _out_ffn_kernel(x_ref, ya_ref, yb_ref, wout_ref, gpost_ref, gpre2_ref, wup_ref, fcw_ref,
                    fcb_ref, wdown_ref, gpost2_ref, o_ref, cf_ref):
    tm = x_ref.shape[1]
    sub = TM_SUB

    @pl.when(pl.program_id(1) == 0)
    def _():
        cf_ref[...] = jnp.zeros_like(cf_ref)

    subs = [pl.ds(i * sub, sub) for i in range(tm // sub)]
    mixes = [_dot(jnp.concatenate([ya_ref[0, rs, :], yb_ref[0, rs, :]], axis=1), wout_ref[...])
             for rs in subs]

    def conv_cols(f, lo):
        cols = slice(lo, lo + FF_COLS)
        prev = cf_ref[:, cols]
        cw = fcw_ref[:, cols]
        out = (cw[0:1] * _shift_rows(f, prev, 2) + cw[1:2] * _shift_rows(f, prev, 1)
               + cw[2:3] * f + fcb_ref[:, cols])
        cf_ref[:, cols] = f[sub - SUBLANES:]
        return out

    n_groups = D_FF // FF_COLS
    hs, ups = [], []
    for mix, rs in zip(mixes, subs):
        h = x_ref[0, rs, :] + _rms(mix, gpost_ref[...])
        hn = _rms(h, gpre2_ref[...]).astype(BF16)
        hs.append(h)
        ups.append([[_dot(hn, wup_ref[:, lo:lo + FF_COLS]) for lo in (j * FF_COLS, D_FF + j * FF_COLS)]
                    for j in range(n_groups)])
    for rs, h, up_s in zip(subs, hs, ups):
        acts = []
        for j, (f_gate, f_up) in enumerate(up_s):
            gate = conv_cols(f_gate, j * FF_COLS)
            up = conv_cols(f_up, D_FF + j * FF_COLS)
            acts.append(_gelu_tanh_times(gate, up).astype(BF16))
        ffn = _dot(jnp.concatenate(acts, axis=1), wdown_ref[...])
        o_ref[0, rs, :] = h + _rms(ffn, gpost2_ref[...])


def _out_ffn(x, ya, yb, wout, gpost, gpre2, wup, fcw, fcb, wdown, gpost2):
    b, t, d = x.shape
    tm = TM_OUT
    tok = lambda n: pl.BlockSpec((1, tm, n), lambda i, j: (i, j, 0))
    full = lambda arr: pl.BlockSpec(arr.shape, lambda i, j: (0,) * arr.ndim)
    params = (wout, gpost, gpre2, wup, fcw, fcb, wdown, gpost2)
    return pl.pallas_call(
        _out_ffn_kernel,
        grid=(b, t // tm),
        in_specs=[tok(d), tok(CONV_DIM), tok(RWKV_DIM)] + [full(p) for p in params],
        out_specs=tok(d),
        out_shape=jax.ShapeDtypeStruct((b, t, d), F32),
        scratch_shapes=[pltpu.VMEM((SUBLANES, 2 * D_FF), F32)],
        compiler_params=pltpu.CompilerParams(
            dimension_semantics=("arbitrary", "arbitrary"), vmem_limit_bytes=VMEM_LIMIT),
        name="out_ffn",
    )(x, ya, yb, *params)


def _row(vec):
    return vec.reshape(1, -1).astype(F32)


def _head_ones():
    head = jnp.arange(HEAD_ONES) // HEAD_DIM
    return (head[:, None] == head[None, :]).astype(BF16)


def _layer(h, pre_mix_g, w_in, conv_a_w, shift_mu, w0, w2, a0, a2, g2, k_k, k_a, r_k, lnx_w, lnx_b,
           w_out, post_mix_g, pre_ffn_g, w_up, ffn_conv_w, ffn_conv_b, w_down, post_ffn_g):
    ca = 3 * CONV_DIM
    rb = ca + 3 * RWKV_DIM
    gate_pad = GATE_PAD - GATE_RANK
    wa = w_in[:, :ca].astype(BF16)
    wb = w_in[:, ca:rb].astype(BF16)
    wc = jnp.pad(w_in[:, rb:], ((0, 0), (0, gate_pad))).astype(BF16)
    mub = _row(shift_mu[:3 * RWKV_DIM])
    muc = _row(jnp.pad(shift_mu[3 * RWKV_DIM:], (0, gate_pad)))
    w2a2 = jnp.zeros((DECAY_RANK + ICLR_RANK, 2 * RWKV_DIM), F32)
    w2a2 = w2a2.at[:DECAY_RANK, :RWKV_DIM].set(w2).at[DECAY_RANK:, RWKV_DIM:].set(a2).astype(BF16)
    g2p = jnp.pad(g2, ((0, gate_pad), (0, 0))).astype(BF16)
    hsum = _head_ones()

    ya, r, k, v, kkn, a, lw, g = _mix_in(
        h, _row(pre_mix_g), wa, wb, wc, mub, muc, conv_a_w.astype(F32), _row(w0), _row(a0), w2a2,
        g2p, _row(k_k), _row(k_a), hsum)
    yb = _wkv(r, k, v, kkn, a, lw, g, _row(lnx_w), _row(lnx_b), _row(r_k))
    return _out_ffn(h, ya, yb, w_out.astype(BF16), _row(post_mix_g), _row(pre_ffn_g),
                    w_up.astype(BF16), ffn_conv_w.astype(F32), _row(ffn_conv_b),
                    w_down.astype(BF16), _row(post_ffn_g))


def kernel(x, pre_mix_g, w_in, conv_a_w, shift_mu, w0, w2, a0, a2, g2, k_k, k_a, r_k, lnx_w, lnx_b,
           w_out, post_mix_g, pre_ffn_g, w_up, ffn_conv_w, ffn_conv_b, w_down, post_ffn_g):
    h = x
    for l in range(pre_mix_g.shape[0]):
        h = _layer(h, pre_mix_g[l], w_in[l], conv_a_w[l], shift_mu[l], w0[l], w2[l], a0[l], a2[l],
                   g2[l], k_k[l], k_a[l], r_k[l], lnx_w[l], lnx_b[l], w_out[l], post_mix_g[l],
                   pre_ffn_g[l], w_up[l], ffn_conv_w[l], ffn_conv_b[l], w_down[l], post_ffn_g[l])
    return h
```
